```python
import jax, jax.numpy as jnp
from jax import lax
import numpy as np

D_MODEL = 1024
BATCH = 1
SEQ = 16384
DEPTH = 4

N_MIXERS = 3
PLE_DIM = 256
D_FF = 2816
EPS = 1e-6
HG_HEADS = 8
HG_DK = D_MODEL // HG_HEADS
HG_DV = D_MODEL // HG_HEADS
HG_CHUNK = 64
ML_HEADS = 4
ML_DQK = D_MODEL // 2 // ML_HEADS
ML_DV = D_MODEL // ML_HEADS
ML_CHUNK = 64
ML_GATE_CAP = 15.0
SW_HEADS = 16
SW_KV_HEADS = 4
SW_GROUP = SW_HEADS // SW_KV_HEADS
SW_HD = 64
WINDOW = 128
SW_BLOCK = 128
N_A = (DEPTH + 2) // 3
N_B = (DEPTH + 1) // 3
N_C = DEPTH // 3

kernel_name = "hybrid_hgrn2_mlstm_swa_macaron"


def rmsnorm(x, gain):
    xf = x.astype(jnp.float32)
    y = xf * lax.rsqrt(jnp.mean(xf * xf, axis=-1, keepdims=True) + EPS)
    return (y * gain.astype(jnp.float32)).astype(x.dtype)


def swiglu(x, w_gu, w_down):
    g, u = jnp.split(x @ w_gu, 2, axis=-1)
    return (jax.nn.silu(g) * u) @ w_down


def to_chunks(t, L):
    B, T, H, d = t.shape
    return t.reshape(B, T // L, L, H, d).transpose(1, 0, 3, 2, 4)


def from_chunks(t):
    N, B, H, L, d = t.shape
    return t.transpose(1, 0, 3, 2, 4).reshape(B, N * L, H, d)


def gate_chunks(t, L):
    B, T, H = t.shape
    return t.reshape(B, T // L, L, H).transpose(1, 0, 3, 2)


def alibi_slopes(n_heads):
    return jnp.asarray([2.0 ** (-8.0 * (h + 1) / n_heads) for h in range(n_heads)], jnp.float32)


def hgrn2_mixer(x, lb, w_in, g_norm, w_out):
    B, T, _ = x.shape
    q, f, i, og = jnp.split(x @ w_in, 4, axis=-1)
    q = jax.nn.silu(q).astype(jnp.float32)
    f = f.astype(jnp.float32)
    lb = lb.astype(jnp.float32)
    log_f = jnp.logaddexp(jnp.log(lb), jnp.log1p(-lb) + jax.nn.log_sigmoid(f))
    k = (1.0 - lb) * jax.nn.sigmoid(-f)
    L = HG_CHUNK
    qc = to_chunks(q.reshape(B, T, HG_HEADS, HG_DK), L)
    kc = to_chunks(k.reshape(B, T, HG_HEADS, HG_DK), L)
    gc = to_chunks(log_f.reshape(B, T, HG_HEADS, HG_DK), L)
    vc = to_chunks(i.astype(jnp.float32).reshape(B, T, HG_HEADS, HG_DV), L)
    causal = jnp.tril(jnp.ones((L, L), bool))

    def step(S, inp):
        q_, k_, v_, g_ = inp
        b = jnp.cumsum(g_, axis=2)
        o_inter = jnp.einsum('bhtk,bhkv->bhtv', q_ * jnp.exp(b), S)
        diff = b[:, :, :, None, :] - b[:, :, None, :, :]
        decay = jnp.exp(jnp.where(causal[:, :, None], diff, -jnp.inf))
        A = jnp.einsum('bhtk,bhsk,bhtsk->bhts', q_, k_, decay)
        o_intra = jnp.einsum('bhts,bhsv->bhtv', A, v_)
        b_last = b[:, :, -1:, :]
        S_new = jnp.exp(b_last[:, :, 0, :])[..., None] * S + jnp.einsum(
            'bhsk,bhsv->bhkv', k_ * jnp.exp(b_last - b), v_)
        return S_new, o_inter + o_intra

    S0 = jnp.zeros((B, HG_HEADS, HG_DK, HG_DV), jnp.float32)
    _, o = lax.scan(step, S0, (qc, kc, vc, gc))
    o = rmsnorm(from_chunks(o), g_norm)
    o = o.reshape(B, T, D_MODEL).astype(x.dtype) * jax.nn.silu(og)
    return o @ w_out


def mlstm_mixer(x, w_qkvo, w_if, b_if, norm_g, w_out):
    B, T, _ = x.shape
    dqk_all = ML_HEADS * ML_DQK
    q, k, v, og = jnp.split(x @ w_qkvo, [dqk_all, 2 * dqk_all, 2 * dqk_all + D_MODEL], axis=-1)
    gates = (x @ w_if).astype(jnp.float32) + b_if.astype(jnp.float32)
    gates = ML_GATE_CAP * jnp.tanh(gates / ML_GATE_CAP)
    ig, fg = jnp.split(gates, 2, axis=-1)
    lf = jax.nn.log_sigmoid(fg)
    L = ML_CHUNK
    qc = to_chunks(q.astype(jnp.float32).reshape(B, T, ML_HEADS, ML_DQK), L)
    kc = to_chunks(k.astype(jnp.float32).reshape(B, T, ML_HEADS, ML_DQK), L) * (ML_DQK ** -0.5)
    vc = to_chunks(v.astype(jnp.float32).reshape(B, T, ML_HEADS, ML_DV), L)
    ic = gate_chunks(ig, L)
    fc = gate_chunks(lf, L)
    causal = jnp.tril(jnp.ones((L, L), bool))

    def step(carry, inp):
        C, n, m = carry
        q_, k_, v_, i_, f_ = inp
        b = jnp.cumsum(f_, axis=-1)
        log_intra = jnp.where(causal, b[..., :, None] - b[..., None, :] + i_[..., None, :], -jnp.inf)
        log_inter = b + m[..., None]
        m_t = jnp.maximum(jnp.max(log_intra, axis=-1), log_inter)
        w_intra = jnp.exp(log_intra - m_t[..., None])
        w_inter = jnp.exp(log_inter - m_t)
        scores = jnp.einsum('bhtk,bhsk->bhts', q_, k_) * w_intra
        num = w_inter[..., None] * jnp.einsum('bhtk,bhkv->bhtv', q_, C) + jnp.einsum('bhts,bhsv->bhtv', scores, v_)
        den = w_inter * jnp.einsum('bhtk,bhk->bht', q_, n) + jnp.sum(scores, axis=-1)
        h = num / jnp.maximum(jnp.abs(den), jnp.exp(-m_t))[..., None]
        b_L = b[..., -1]
        log_state = b_L + m
        log_src = b_L[..., None] - b + i_
        m_new = jnp.maximum(log_state, jnp.max(log_src, axis=-1))
        w_src = jnp.exp(log_src - m_new[..., None])
        decay = jnp.exp(log_state - m_new)
        C_new = decay[..., None, None] * C + jnp.einsum('bhsk,bhsv->bhkv', k_ * w_src[..., None], v_)
        n_new = decay[..., None] * n + jnp.einsum('bhs,bhsk->bhk', w_src, k_)
        return (C_new, n_new, m_new), h

    carry0 = (jnp.zeros((B, ML_HEADS, ML_DQK, ML_DV), jnp.float32),
              jnp.zeros((B, ML_HEADS, ML_DQK), jnp.float32),
              jnp.zeros((B, ML_HEADS), jnp.float32))
    _, h = lax.scan(step, carry0, (qc, kc, vc, ic, fc))
    h = rmsnorm(from_chunks(h), norm_g.reshape(ML_HEADS, ML_DV))
    h = h.reshape(B, T, D_MODEL).astype(x.dtype) * jax.nn.sigmoid(og)
    return h @ w_out


def swa_mixer(x, w_qkv, q_gain, k_gain, sinks, w_o):
    B, T, _ = x.shape
    nq = SW_HEADS * SW_HD
    nk = SW_KV_HEADS * SW_HD
    q, k, v = jnp.split(x @ w_qkv, [nq, nq + nk], axis=-1)
    q = rmsnorm(q.reshape(B, T, SW_HEADS, SW_HD), q_gain)
    k = rmsnorm(k.reshape(B, T, SW_KV_HEADS, SW_HD), k_gain)
    v = v.reshape(B, T, SW_KV_HEADS, SW_HD)
    nb = T // SW_BLOCK
    qb = q.reshape(B, nb, SW_BLOCK, SW_KV_HEADS, SW_GROUP, SW_HD)

    def band(t):
        tb = t.reshape(B, nb, SW_BLOCK, SW_KV_HEADS, SW_HD)
        prev = jnp.pad(tb, ((0, 0), (1, 0), (0, 0), (0, 0), (0, 0)))[:, :-1]
        return jnp.concatenate([prev, tb], axis=2)

    kk, vv = band(k), band(v)
    s = jnp.einsum('bnqkgd,bnskd->bnkgqs', qb, kk).astype(jnp.float32) * (SW_HD ** -0.5)
    qpos = jnp.arange(SW_BLOCK)
    kpos = jnp.arange(2 * SW_BLOCK) - SW_BLOCK
    dist = (qpos[:, None] - kpos[None, :]).astype(jnp.float32)
    blk_start = jnp.arange(nb) * SW_BLOCK
    valid = (dist >= 0) & (dist < WINDOW) & ((blk_start[:, None, None] + kpos[None, None, :]) >= 0)
    slopes = alibi_slopes(SW_HEADS).reshape(SW_KV_HEADS, SW_GROUP)
    s = s - slopes[:, :, None, None] * dist
    s = jnp.where(valid[None, :, None, None], s, -jnp.inf)
    sink = jnp.broadcast_to(sinks.astype(jnp.float32).reshape(SW_KV_HEADS, SW_GROUP)[:, :, None, None],
                            s.shape[:-1] + (1,))
    probs = jax.nn.softmax(jnp.concatenate([s, sink], axis=-1), axis=-1)[..., :-1]
    o = jnp.einsum('bnkgqs,bnskd->bnqkgd', probs.astype(vv.dtype), vv).reshape(B, T, nq)
    return o @ w_o


def setup_inputs(seed: int = 0) -> dict:
    key = jax.random.key(seed)
    ks = jax.random.split(key, 24)
    f32 = jnp.float32

    def nrm(k, shape, fan_in):
        return jax.random.normal(k, shape, f32) * (fan_in ** -0.5)

    D = D_MODEL
    b_if = jnp.concatenate([
        0.1 * jax.random.normal(ks[11], (N_B, ML_HEADS), f32),
        jnp.linspace(3.0, 6.0, ML_HEADS, dtype=f32)[None, :] + 0.1 * jax.random.normal(ks[12], (N_B, ML_HEADS), f32),
    ], axis=-1)
    return {
        "x": jax.random.normal(ks[0], (BATCH, SEQ, D), f32),
        "p": jax.random.normal(ks[1], (DEPTH, BATCH, SEQ, PLE_DIM), f32),
        "norm_gains": 1.0 + 0.02 * jax.random.normal(ks[2], (DEPTH, 4, D), f32),
        "w_ffn_gu": nrm(ks[3], (DEPTH, 2, D, 2 * D_FF), D),
        "w_ffn_down": nrm(ks[4], (DEPTH, 2, D_FF, D), D_FF),
        "w_ple_gate": nrm(ks[5], (DEPTH, D, D), D),
        "w_ple_proj": nrm(ks[6], (DEPTH, PLE_DIM, D), PLE_DIM),
        "hg_lower_bounds": 0.5 * jax.random.normal(ks[7], (DEPTH, D), f32),
        "hg_w_in": nrm(ks[8], (N_A, D, 4 * D), D),
        "hg_g_norm": 1.0 + 0.02 * jax.random.normal(ks[9], (N_A, HG_DV), f32),
        "hg_w_out": nrm(ks[10], (N_A, D, D), D),
        "ml_w_qkvo": nrm(ks[13], (N_B, D, 2 * ML_HEADS * ML_DQK + 2 * D), D),
        "ml_w_if": nrm(ks[14], (N_B, D, 2 * ML_HEADS), D),
        "ml_b_if": b_if,
        "ml_norm": 1.0 + 0.02 * jax.random.normal(ks[15], (N_B, D), f32),
        "ml_w_out": nrm(ks[16], (N_B, D, D), D),
        "sw_w_qkv": nrm(ks[17], (N_C, D, (SW_HEADS + 2 * SW_KV_HEADS) * SW_HD), D),
        "sw_q_norm": 1.0 + 0.02 * jax.random.normal(ks[18], (N_C, SW_HD), f32),
        "sw_k_norm": 1.0 + 0.02 * jax.random.normal(ks[19], (N_C, SW_HD), f32),
        "sw_sinks": 0.5 * jax.random.normal(ks[20], (N_C, SW_HEADS), f32),
        "sw_w_o": nrm(ks[21], (N_C, SW_HEADS * SW_HD, D), SW_HEADS * SW_HD),
    }


def reference(x, p, norm_gains, w_ffn_gu, w_ffn_down, w_ple_gate, w_ple_proj,
              hg_lower_bounds, hg_w_in, hg_g_norm, hg_w_out,
              ml_w_qkvo, ml_w_if, ml_b_if, ml_norm, ml_w_out,
              sw_w_qkv, sw_q_norm, sw_k_norm, sw_sinks, sw_w_o):
    lbs = jnp.cumsum(jax.nn.softmax(hg_lower_bounds.astype(jnp.float32), axis=0), axis=0)
    lbs = lbs - lbs[0:1]
    for layer in range(DEPTH):
        g = norm_gains[layer]
        h = x + 0.5 * swiglu(rmsnorm(x, g[0]), w_ffn_gu[layer, 0], w_ffn_down[layer, 0])
        xn = rmsnorm(h, g[1])
        kind = layer % N_MIXERS
        j = layer // N_MIXERS
        if kind == 0:
            mix = hgrn2_mixer(xn, lbs[layer], hg_w_in[j], hg_g_norm[j], hg_w_out[j])
        elif kind == 1:
            mix = mlstm_mixer(xn, ml_w_qkvo[j], ml_w_if[j], ml_b_if[j], ml_norm[j], ml_w_out[j])
        else:
            mix = swa_mixer(xn, sw_w_qkv[j], sw_q_norm[j], sw_k_norm[j], sw_sinks[j], sw_w_o[j])
        h = h + mix
        h = h + 0.5 * swiglu(rmsnorm(h, g[2]), w_ffn_gu[layer, 1], w_ffn_down[layer, 1])
        gate = jax.nn.sigmoid(rmsnorm(h, g[3]) @ w_ple_gate[layer])
        x = h + gate * (p[layer] @ w_ple_proj[layer])
    return x
```

```python
import functools

import numpy as np
import jax
import jax.numpy as jnp
from jax import lax
from jax.experimental import pallas as pl
from jax.experimental.pallas import tpu as pltpu

F32 = jnp.float32
BF16 = jnp.bfloat16

EPS = 1e-6
N_MIXERS = 3
HG_HEADS = 8
ML_HEADS = 4
ML_GATE_CAP = 15.0
SW_HEADS = 16
SW_KV_HEADS = 4
SW_HD = 64
SW_WINDOW = 128

V7X_VMEM_LIMIT_BYTES = 56 * 1024 * 1024
LANES = 128

ROW_TILE = 256
FF_CHUNK = 256
HG_CHUNK = 128
ML_CHUNK = 256
SW_BLOCK = 128


def _dot(a, b):
    return jnp.dot(a, b, preferred_element_type=F32)


def _dot_nt(a, b):
    return lax.dot_general(a, b, (((1,), (1,)), ((), ())), preferred_element_type=F32)


def _dot_tn(a, b):
    return lax.dot_general(a, b, (((0,), (0,)), ((), ())), preferred_element_type=F32)


def _sigmoid(x):
    return 1.0 / (1.0 + jnp.exp(-x))


def _log_sigmoid(x):
    return jnp.minimum(x, 0.0) - jnp.log1p(jnp.exp(-jnp.abs(x)))


def _rms(x, gain):
    return x * lax.rsqrt(jnp.mean(x * x, axis=-1, keepdims=True) + EPS) * gain


def _split_bf16(x, terms):
    parts = []
    r = x
    for _ in range(terms):
        p = r.astype(BF16)
        parts.append(p)
        r = r - p.astype(F32)
    return parts


def _swiglu_residual(x, gain, wgu_ref, wd_ref):
    d_ff = wd_ref.shape[0]
    xn = _rms(x, gain).astype(BF16)
    y = jnp.zeros_like(x)
    for j in range(d_ff // FF_CHUNK):
        lo = j * FF_CHUNK
        g = _dot(xn, wgu_ref[:, lo:lo + FF_CHUNK])
        u = _dot(xn, wgu_ref[:, d_ff + lo:d_ff + lo + FF_CHUNK])
        a = (g * _sigmoid(g) * u).astype(BF16)
        y = y + _dot(a, wd_ref[lo:lo + FF_CHUNK, :])
    return x + 0.5 * y


def _resident(shape):
    nd = len(shape)
    return pl.BlockSpec(shape, lambda *_: (0,) * nd, pipeline_mode=pl.Buffered(1))


def _params(semantics):
    return pltpu.CompilerParams(dimension_semantics=semantics,
                                vmem_limit_bytes=V7X_VMEM_LIMIT_BYTES)


def _pre_kernel(*refs, splits, n_gate):
    if n_gate:
        (x_ref, gains_ref, wgu_ref, wd_ref, win_ref, wif_ref, bif_ref), outs = refs[:7], refs[7:]
    else:
        (x_ref, gains_ref, wgu_ref, wd_ref, win_ref), outs = refs[:5], refs[5:]
    h_ref = outs[0]
    proj_refs = outs[1:1 + len(splits)]
    h = _swiglu_residual(x_ref[...], gains_ref[0:1, :], wgu_ref, wd_ref)
    h_ref[...] = h
    xn = _rms(h, gains_ref[1:2, :])
    xb = xn.astype(BF16)
    off = 0
    for r, n in zip(proj_refs, splits):
        r[...] = _dot(xb, win_ref[:, off:off + n]).astype(r.dtype)
        off += n
    if n_gate:
        g_ref, gt_ref = outs[1 + len(splits):]
        xh, xl = _split_bf16(xn, 2)
        wh, wl = _split_bf16(wif_ref[...], 2)
        pre = _dot(xh, wh) + _dot(xl, wh) + _dot(xh, wl) + bif_ref[...]
        capped = ML_GATE_CAP * jnp.tanh(pre / ML_GATE_CAP)
        lane = lax.broadcasted_iota(jnp.int32, capped.shape, 1)
        gates = jnp.where(lane < n_gate // 2, capped, _log_sigmoid(capped))
        g_ref[...] = gates
        gt_ref[...] = gates.T[0:gt_ref.shape[0], :]


def _pre_call(x, gains, wgu, wd, win, splits, out_dtypes, gate_w=None, gate_b=None):
    t, d = x.shape
    n_gate = 0 if gate_w is None else gate_w.shape[1]
    tm = ROW_TILE
    row = lambda n: pl.BlockSpec((tm, n), lambda i: (i, 0))
    in_specs = [row(d), _resident(gains.shape), _resident(wgu.shape), _resident(wd.shape),
                _resident(win.shape)]
    args = [x, gains, wgu, wd, win]
    out_shape = [jax.ShapeDtypeStruct((t, d), F32)]
    out_specs = [row(d)]
    for n, dt in zip(splits, out_dtypes):
        out_shape.append(jax.ShapeDtypeStruct((t, n), dt))
        out_specs.append(row(n))
    if n_gate:
        wpad = jnp.zeros((d, LANES), F32).at[:, :n_gate].set(gate_w)
        bpad = jnp.zeros((1, LANES), F32).at[0, :n_gate].set(gate_b)
        in_specs += [_resident(wpad.shape), _resident(bpad.shape)]
        args += [wpad, bpad]
        out_shape += [jax.ShapeDtypeStruct((t, LANES), F32), jax.ShapeDtypeStruct((n_gate, t), F32)]
        out_specs += [row(LANES), pl.BlockSpec((n_gate, tm), lambda i: (0, i))]
    return pl.pallas_call(
        functools.partial(_pre_kernel, splits=tuple(splits), n_gate=n_gate),
        grid=(t // tm,), in_specs=in_specs, out_specs=out_specs, out_shape=out_shape,
        compiler_params=_params(("parallel",)), name="pre",
    )(*args)


def _post_kernel(h_ref, o_ref, p_ref, gains_ref, wout_ref, wgu_ref, wd_ref, wpg_ref, wpp_ref, x_ref):
    h = h_ref[...] + _dot(o_ref[...], wout_ref[...])
    h = _swiglu_residual(h, gains_ref[2:3, :], wgu_ref, wd_ref)
    gate = _sigmoid(_dot(_rms(h, gains_ref[3:4, :]).astype(BF16), wpg_ref[...]))
    x_ref[...] = h + gate * _dot(p_ref[...].astype(BF16), wpp_ref[...])


def _post_call(h, o, p, gains, wout, wgu, wd, wpg, wpp):
    t, d = h.shape
    tm = ROW_TILE
    row = lambda n: pl.BlockSpec((tm, n), lambda i: (i, 0))
    weights = [gains, wout, wgu, wd, wpg, wpp]
    return pl.pallas_call(
        _post_kernel, grid=(t // tm,),
        in_specs=[row(d), row(o.shape[1]), row(p.shape[1])] + [_resident(w.shape) for w in weights],
        out_specs=row(d), out_shape=jax.ShapeDtypeStruct((t, d), F32),
        compiler_params=_params(("parallel",)), name="post",
    )(h, o, p, *weights)


def _hgrn2_tables(c):
    n_lev = int(np.log2(c))
    assert 2 ** n_lev == c
    mats = np.zeros((n_lev + 2, c, c), np.float32)
    r = np.arange(c)[:, None]
    j = np.arange(c)[None, :]
    mats[0] = j <= r
    for lev in range(n_lev):
        half = 2 ** lev
        m = (r // (2 * half)) * (2 * half) + half
        mats[1 + lev] = np.where(r >= m, (j >= m) & (j <= r), (j > r) & (j <= m - 1))
    mats[n_lev + 1] = j > r
    x = r ^ j
    lvl = np.where(r > j, np.floor(np.log2(np.maximum(x, 1))).astype(np.int32), -1)
    lvl = np.where(r == j, n_lev, lvl).astype(np.int32)
    return jnp.asarray(mats.reshape(-1, c), BF16), jnp.asarray(lvl), n_lev


def _hgrn2_kernel(lbraw_ref, gn_ref, mexp_ref, lvl_ref, q_ref, f_ref, i_ref, og_ref, o_ref, st_ref,
                  *, layer, n_lev, heads):
    c, d = q_ref.shape
    dk = d // heads

    @pl.when(pl.program_id(0) == 0)
    def _():
        st_ref[...] = jnp.zeros_like(st_ref)

    rows = [lbraw_ref[r:r + 1, :] for r in range(lbraw_ref.shape[0])]
    mx = functools.reduce(jnp.maximum, rows)
    es = [jnp.exp(r - mx) for r in rows]
    lb = sum(es[1:layer + 1], jnp.zeros_like(mx)) / sum(es)

    f = f_ref[...]
    a = jnp.log(lb)
    b = jnp.log1p(-lb) + _log_sigmoid(f)
    logf = jnp.maximum(a, b) + jnp.log1p(jnp.exp(-jnp.abs(a - b)))
    key = (1.0 - lb) * _sigmoid(-f)

    g_hi, g_lo = _split_bf16(logf, 2)
    mexp = mexp_ref[...]
    expo = _dot(mexp, g_hi) + _dot(mexp, g_lo)

    lvl = lvl_ref[...]
    for h in range(heads):
        hs = slice(h * dk, (h + 1) * dk)
        qv = q_ref[:, hs]
        qh = qv * _sigmoid(qv)
        kh = key[:, hs]
        vh = i_ref[:, hs].astype(BF16)
        bh = expo[0:c, hs]
        st = st_ref[h]
        o = _dot_nt((qh * jnp.exp(bh)).astype(BF16), st.astype(BF16))
        att = jnp.where(lvl == n_lev, _dot_nt(qh.astype(BF16), kh.astype(BF16)), 0.0)
        for lev in range(n_lev):
            e = jnp.exp(expo[(1 + lev) * c:(2 + lev) * c, hs])
            att = jnp.where(lvl == lev, _dot_nt((qh * e).astype(BF16), (kh * e).astype(BF16)), att)
        o = o + _dot(att.astype(BF16), vh)
        khat = (kh * jnp.exp(expo[(n_lev + 1) * c:(n_lev + 2) * c, hs])).astype(BF16)
        st_ref[h] = st * jnp.exp(bh[c - 1:c, :]) + _dot_tn(vh, khat)
        ogv = og_ref[:, hs]
        o_ref[:, hs] = (_rms(o, gn_ref[...]) * (ogv * _sigmoid(ogv))).astype(o_ref.dtype)


def _hgrn2_call(q, f, i, og, lbraw, g_norm, layer):
    t, d = q.shape
    c = HG_CHUNK
    mexp, lvl, n_lev = _hgrn2_tables(c)
    dk = d // HG_HEADS
    blk = pl.BlockSpec((c, d), lambda n: (n, 0))
    return pl.pallas_call(
        functools.partial(_hgrn2_kernel, layer=layer, n_lev=n_lev, heads=HG_HEADS),
        grid=(t // c,),
        in_specs=[_resident(lbraw.shape), _resident(g_norm.shape), _resident(mexp.shape),
                  _resident(lvl.shape), blk, blk, blk, blk],
        out_specs=blk, out_shape=jax.ShapeDtypeStruct((t, d), BF16),
        scratch_shapes=[pltpu.VMEM((HG_HEADS, dk, dk), F32)],
        compiler_params=_params(("arbitrary",)), name="hgrn2",
    )(lbraw, g_norm, mexp, lvl, q, f, i, og)


def _mlstm_kernel(tril_ref, norm_ref, q_ref, k_ref, v_ref, og_ref, g_ref, gt_ref, o_ref,
                  c_ref, n_ref, m_ref, *, heads):
    c = q_ref.shape[0]
    dqk = q_ref.shape[1] // heads
    dv = v_ref.shape[1] // heads

    @pl.when(pl.program_id(0) == 0)
    def _():
        c_ref[...] = jnp.zeros_like(c_ref)
        n_ref[...] = jnp.zeros_like(n_ref)
        m_ref[...] = jnp.zeros_like(m_ref)

    tril = tril_ref[...]
    gates = g_ref[...]
    gates_t = gt_ref[...]
    cum_col = sum(_dot(tril, part) for part in _split_bf16(gates, 3))
    cum_row = sum(_dot_nt(part, tril) for part in _split_bf16(gates_t, 3))
    causal = lax.broadcasted_iota(jnp.int32, (c, c), 0) >= lax.broadcasted_iota(jnp.int32, (c, c), 1)

    for h in range(heads):
        b_col = cum_col[:, heads + h:heads + h + 1]
        b_row = cum_row[heads + h:heads + h + 1, :]
        i_col = gates[:, h:h + 1]
        i_row = gates_t[h:h + 1, :]
        m_prev = m_ref[h:h + 1, 0:1]
        qh = q_ref[:, h * dqk:(h + 1) * dqk]
        kh = k_ref[:, h * dqk:(h + 1) * dqk] * (dqk ** -0.5)
        vh = v_ref[:, h * dv:(h + 1) * dv].astype(BF16)
        cst = c_ref[h]
        nst = n_ref[h:h + 1, :]

        log_intra = jnp.where(causal, b_col - b_row + i_row, -jnp.inf)
        log_inter = b_col + m_prev
        m_t = jnp.maximum(jnp.max(log_intra, axis=1, keepdims=True), log_inter)
        w_intra = jnp.exp(log_intra - m_t)
        w_inter = jnp.exp(log_inter - m_t)
        qb = qh.astype(BF16)
        scores = _dot_nt(qb, kh.astype(BF16)) * w_intra
        num = w_inter * _dot(qb, cst.astype(BF16)) + _dot(scores.astype(BF16), vh)
        den = w_inter * jnp.sum(qh * nst, axis=1, keepdims=True) + jnp.sum(scores, axis=1, keepdims=True)
        hid = num / jnp.maximum(jnp.abs(den), jnp.exp(-m_t))

        b_last = b_col[c - 1:c, :]
        log_state = b_last + m_prev
        m_new = jnp.maximum(log_state, jnp.max(b_last - b_row + i_row, axis=1, keepdims=True))
        w_src = jnp.exp(b_last - b_col + i_col - m_new)
        decay = jnp.exp(log_state - m_new)
        khat = kh * w_src
        c_ref[h] = decay * cst + _dot_tn(khat.astype(BF16), vh)
        n_ref[h:h + 1, :] = decay * nst + jnp.sum(khat, axis=0, keepdims=True)
        m_ref[h:h + 1, :] = jnp.broadcast_to(m_new, (1, m_ref.shape[1]))

        ogv = og_ref[:, h * dv:(h + 1) * dv]
        o_ref[:, h * dv:(h + 1) * dv] = (
            _rms(hid, norm_ref[:, h * dv:(h + 1) * dv]) * _sigmoid(ogv)).astype(o_ref.dtype)


def _mlstm_call(q, k, v, og, gates, gates_t, norm):
    t = q.shape[0]
    d = v.shape[1]
    c = ML_CHUNK
    dqk = q.shape[1] // ML_HEADS
    dv = d // ML_HEADS
    tril = jnp.asarray(np.tril(np.ones((c, c), np.float32)), BF16)
    blk = lambda n: pl.BlockSpec((c, n), lambda s: (s, 0))
    return pl.pallas_call(
        functools.partial(_mlstm_kernel, heads=ML_HEADS),
        grid=(t // c,),
        in_specs=[_resident(tril.shape), _resident(norm.shape), blk(q.shape[1]), blk(k.shape[1]),
                  blk(d), blk(d), blk(LANES), pl.BlockSpec((2 * ML_HEADS, c), lambda s: (0, s))],
        out_specs=blk(d), out_shape=jax.ShapeDtypeStruct((t, d), BF16),
        scratch_shapes=[pltpu.VMEM((ML_HEADS, dqk, dv), F32), pltpu.VMEM((8, dqk), F32),
                        pltpu.VMEM((8, LANES), F32)],
        compiler_params=_params(("arbitrary",)), name="mlstm",
    )(tril, norm, q, k, v, og, gates, gates_t)


def _swa_kernel(qg_ref, kg_ref, sink_ref, q_ref, kp_ref, kc_ref, vp_ref, vc_ref, o_ref):
    blk = q_ref.shape[0]
    group = SW_HEADS // SW_KV_HEADS
    kk = jnp.concatenate([kp_ref[...], kc_ref[...]], axis=0)
    vv = jnp.concatenate([vp_ref[...], vc_ref[...]], axis=0)
    qpos = lax.broadcasted_iota(jnp.int32, (blk, 2 * blk), 0)
    kpos = lax.broadcasted_iota(jnp.int32, (blk, 2 * blk), 1) - blk
    dist = qpos - kpos
    valid = (dist >= 0) & (dist < SW_WINDOW) & ((kpos >= 0) | (pl.program_id(0) > 0))
    distf = dist.astype(F32)
    outs = []
    for kv in range(SW_KV_HEADS):
        ks = slice(kv * SW_HD, (kv + 1) * SW_HD)
        kn = _rms(kk[:, ks], kg_ref[...]).astype(BF16)
        vb = vv[:, ks].astype(BF16)
        for g in range(group):
            h = kv * group + g
            slope = 2.0 ** (-8.0 * (h + 1) / SW_HEADS)
            qn = (_rms(q_ref[:, h * SW_HD:(h + 1) * SW_HD], qg_ref[...]) * (SW_HD ** -0.5)).astype(BF16)
            s = _dot_nt(qn, kn) - slope * distf
            s = jnp.where(valid, s, -jnp.inf)
            sink = sink_ref[0:1, h:h + 1]
            mx = jnp.maximum(jnp.max(s, axis=1, keepdims=True), sink)
            p = jnp.exp(s - mx)
            den = jnp.sum(p, axis=1, keepdims=True) + jnp.exp(sink - mx)
            outs.append(_dot(p.astype(BF16), vb) / den)
    o_ref[...] = jnp.concatenate(outs, axis=1).astype(o_ref.dtype)


def _swa_call(q, k, v, q_gain, k_gain, sinks):
    t, nq = q.shape
    nk = k.shape[1]
    b = SW_BLOCK
    cur = lambda n: pl.BlockSpec((b, n), lambda s: (s, 0))
    prev = lambda n: pl.BlockSpec((b, n), lambda s: (jnp.maximum(s - 1, 0), 0))
    return pl.pallas_call(
        _swa_kernel, grid=(t // b,),
        in_specs=[_resident(q_gain.shape), _resident(k_gain.shape), _resident(sinks.shape),
                  cur(nq), prev(nk), cur(nk), prev(nk), cur(nk)],
        out_specs=cur(nq), out_shape=jax.ShapeDtypeStruct((t, nq), BF16),
        compiler_params=_params(("parallel",)), name="swa",
    )(q_gain, k_gain, sinks, q, k, k, v, v)


def kernel(x, p, norm_gains, w_ffn_gu, w_ffn_down, w_ple_gate, w_ple_proj, hg_lower_bounds, hg_w_in,
           hg_g_norm, hg_w_out, ml_w_qkvo, ml_w_if, ml_b_if, ml_norm, ml_w_out, sw_w_qkv, sw_q_norm,
           sw_k_norm, sw_sinks, sw_w_o):
    batch, seq, d = x.shape
    depth = p.shape[0]
    outs = []
    for bi in range(batch):
        xs = x[bi]
        for layer in range(depth):
            kind, j = layer % N_MIXERS, layer // N_MIXERS
            gains = norm_gains[layer]
            wgu = w_ffn_gu[layer].astype(BF16)
            wd = w_ffn_down[layer].astype(BF16)
            if kind == 0:
                h, q, f, i, og = _pre_call(xs, gains, wgu[0], wd[0], hg_w_in[j].astype(BF16),
                                           (d, d, d, d), (F32, F32, F32, F32))
                mix = _hgrn2_call(q, f, i, og, hg_lower_bounds, hg_g_norm[j][None, :], layer)
                wout = hg_w_out[j]
            elif kind == 1:
                dqk = (ml_w_qkvo.shape[2] - 2 * d) // 2
                h, q, k, v, og, gates, gates_t = _pre_call(
                    xs, gains, wgu[0], wd[0], ml_w_qkvo[j].astype(BF16), (dqk, dqk, d, d),
                    (F32, F32, F32, F32), gate_w=ml_w_if[j], gate_b=ml_b_if[j])
                mix = _mlstm_call(q, k, v, og, gates, gates_t, ml_norm[j][None, :])
                wout = ml_w_out[j]
            else:
                nq = SW_HEADS * SW_HD
                nk = SW_KV_HEADS * SW_HD
                h, q, k, v = _pre_call(xs, gains, wgu[0], wd[0], sw_w_qkv[j].astype(BF16),
                                       (nq, nk, nk), (F32, F32, F32))
                mix = _swa_call(q, k, v, sw_q_norm[j][None, :], sw_k_norm[j][None, :], sw_sinks[j][None, :])
                wout = sw_w_o[j]
            xs = _post_call(h, mix, p[layer, bi], gains, wout.astype(BF16), wgu[1], wd[1],
                            w_ple_gate[layer].astype(BF16), w_ple_proj[layer].astype(BF16))
        outs.append(xs)
    return jnp.stack(outs, axis=0)
```

```python
import functools

import numpy as np
import jax
import jax.numpy as jnp
from jax import lax
from jax.experimental import pallas as pl
from jax.experimental.pallas import tpu as pltpu

F32 = jnp.float32
BF16 = jnp.bfloat16

EPS = 1e-6
N_MIXERS = 3
HG_HEADS = 8
ML_HEADS = 4
ML_GATE_CAP = 15.0
SW_HEADS = 16
SW_KV_HEADS = 4
SW_HD = 64
SW_WINDOW = 128

V7X_VMEM_LIMIT_BYTES = 56 * 1024 * 1024
LANES = 128

ROW_TILE = 512
FF_CHUNK = 256
HG_CHUNK = 128
ML_CHUNK = 256
SW_BLOCK = 128


def _dot(a, b):
    return jnp.dot(a, b, preferred_element_type=F32)


def _dot_nt(a, b):
    return lax.dot_general(a, b, (((1,), (1,)), ((), ())), preferred_element_type=F32)


def _dot_tn(a, b):
    return lax.dot_general(a, b, (((0,), (0,)), ((), ())), preferred_element_type=F32)


def _sigmoid(x):
    return 1.0 / (1.0 + jnp.exp(-x))


def _log_sigmoid(x):
    return jnp.minimum(x, 0.0) - jnp.log1p(jnp.exp(-jnp.abs(x)))


def _rms(x, gain):
    return x * lax.rsqrt(jnp.mean(x * x, axis=-1, keepdims=True) + EPS) * gain


def _split_bf16(x, terms):
    parts = []
    r = x
    for _ in range(terms):
        p = r.astype(BF16)
        parts.append(p)
        r = r - p.astype(F32)
    return parts


def _swiglu_residual(x, gain, wgu_ref, wd_ref):
    d_ff = wd_ref.shape[0]
    xn = _rms(x, gain).astype(BF16)
    y = jnp.zeros_like(x)
    for j in range(d_ff // FF_CHUNK):
        lo = j * FF_CHUNK
        g = _dot(xn, wgu_ref[:, lo:lo + FF_CHUNK])
        u = _dot(xn, wgu_ref[:, d_ff + lo:d_ff + lo + FF_CHUNK])
        a = (g * _sigmoid(g) * u).astype(BF16)
        y = y + _dot(a, wd_ref[lo:lo + FF_CHUNK, :])
    return x + 0.5 * y


def _resident(shape):
    nd = len(shape)
    return pl.BlockSpec(shape, lambda *_: (0,) * nd, pipeline_mode=pl.Buffered(1))


def _params(semantics):
    return pltpu.CompilerParams(dimension_semantics=semantics,
                                vmem_limit_bytes=V7X_VMEM_LIMIT_BYTES)


def _pre_kernel(*refs, splits, n_gate):
    if n_gate:
        (x_ref, gains_ref, wgu_ref, wd_ref, win_ref, wif_ref, bif_ref), outs = refs[:7], refs[7:]
    else:
        (x_ref, gains_ref, wgu_ref, wd_ref, win_ref), outs = refs[:5], refs[5:]
    h_ref = outs[0]
    proj_refs = outs[1:1 + len(splits)]
    h = _swiglu_residual(x_ref[...], gains_ref[0:1, :], wgu_ref, wd_ref)
    h_ref[...] = h
    xn = _rms(h, gains_ref[1:2, :])
    xb = xn.astype(BF16)
    off = 0
    for r, n in zip(proj_refs, splits):
        r[...] = _dot(xb, win_ref[:, off:off + n]).astype(r.dtype)
        off += n
    if n_gate:
        g_ref, gt_ref = outs[1 + len(splits):]
        xh, xl = _split_bf16(xn, 2)
        wh, wl = _split_bf16(wif_ref[...], 2)
        pre = _dot(xh, wh) + _dot(xl, wh) + _dot(xh, wl) + bif_ref[...]
        capped = ML_GATE_CAP * jnp.tanh(pre / ML_GATE_CAP)
        lane = lax.broadcasted_iota(jnp.int32, capped.shape, 1)
        gates = jnp.where(lane < n_gate // 2, capped, _log_sigmoid(capped))
        g_ref[...] = gates
        gt_ref[...] = gates.T[0:gt_ref.shape[0], :]


def _pre_call(x, gains, wgu, wd, win, splits, out_dtypes, gate_w=None, gate_b=None):
    t, d = x.shape
    n_gate = 0 if gate_w is None else gate_w.shape[1]
    tm = ROW_TILE
    row = lambda n: pl.BlockSpec((tm, n), lambda i: (i, 0))
    in_specs = [row(d), _resident(gains.shape), _resident(wgu.shape), _resident(wd.shape),
                _resident(win.shape)]
    args = [x, gains, wgu, wd, win]
    out_shape = [jax.ShapeDtypeStruct((t, d), F32)]
    out_specs = [row(d)]
    for n, dt in zip(splits, out_dtypes):
        out_shape.append(jax.ShapeDtypeStruct((t, n), dt))
        out_specs.append(row(n))
    if n_gate:
        wpad = jnp.zeros((d, LANES), F32).at[:, :n_gate].set(gate_w)
        bpad = jnp.zeros((1, LANES), F32).at[0, :n_gate].set(gate_b)
        in_specs += [_resident(wpad.shape), _resident(bpad.shape)]
        args += [wpad, bpad]
        out_shape += [jax.ShapeDtypeStruct((t, LANES), F32), jax.ShapeDtypeStruct((n_gate, t), F32)]
        out_specs += [row(LANES), pl.BlockSpec((n_gate, tm), lambda i: (0, i))]
    return pl.pallas_call(
        functools.partial(_pre_kernel, splits=tuple(splits), n_gate=n_gate),
        grid=(t // tm,), in_specs=in_specs, out_specs=out_specs, out_shape=out_shape,
        compiler_params=_params(("parallel",)), name="pre",
    )(*args)


def _post_kernel(h_ref, o_ref, p_ref, gains_ref, wout_ref, wgu_ref, wd_ref, wpg_ref, wpp_ref, x_ref):
    h = h_ref[...] + _dot(o_ref[...], wout_ref[...])
    h = _swiglu_residual(h, gains_ref[2:3, :], wgu_ref, wd_ref)
    gate = _sigmoid(_dot(_rms(h, gains_ref[3:4, :]).astype(BF16), wpg_ref[...]))
    x_ref[...] = h + gate * _dot(p_ref[...].astype(BF16), wpp_ref[...])


def _post_call(h, o, p, gains, wout, wgu, wd, wpg, wpp):
    t, d = h.shape
    tm = ROW_TILE
    row = lambda n: pl.BlockSpec((tm, n), lambda i: (i, 0))
    weights = [gains, wout, wgu, wd, wpg, wpp]
    return pl.pallas_call(
        _post_kernel, grid=(t // tm,),
        in_specs=[row(d), row(o.shape[1]), row(p.shape[1])] + [_resident(w.shape) for w in weights],
        out_specs=row(d), out_shape=jax.ShapeDtypeStruct((t, d), F32),
        compiler_params=_params(("parallel",)), name="post",
    )(h, o, p, *weights)


def _hgrn2_tables(c):
    n_lev = int(np.log2(c))
    assert 2 ** n_lev == c
    mats = np.zeros((n_lev + 2, c, c), np.float32)
    r = np.arange(c)[:, None]
    j = np.arange(c)[None, :]
    mats[0] = j <= r
    for lev in range(n_lev):
        half = 2 ** lev
        m = (r // (2 * half)) * (2 * half) + half
        mats[1 + lev] = np.where(r >= m, (j >= m) & (j <= r), (j > r) & (j <= m - 1))
    mats[n_lev + 1] = j > r
    x = r ^ j
    lvl = np.where(r > j, np.floor(np.log2(np.maximum(x, 1))).astype(np.int32), -1)
    lvl = np.where(r == j, n_lev, lvl).astype(np.int32)
    return jnp.asarray(mats.reshape(-1, c), BF16), jnp.asarray(lvl), n_lev


def _hgrn2_kernel(lbraw_ref, gn_ref, mexp_ref, lvl_ref, q_ref, f_ref, i_ref, og_ref, o_ref, st_ref,
                  *, layer, n_lev, heads):
    c, d = q_ref.shape
    dk = d // heads

    @pl.when(pl.program_id(0) == 0)
    def _():
        st_ref[...] = jnp.zeros_like(st_ref)

    rows = [lbraw_ref[r:r + 1, :] for r in range(lbraw_ref.shape[0])]
    mx = functools.reduce(jnp.maximum, rows)
    es = [jnp.exp(r - mx) for r in rows]
    lb = sum(es[1:layer + 1], jnp.zeros_like(mx)) / sum(es)

    f = f_ref[...]
    a = jnp.log(lb)
    b = jnp.log1p(-lb) + _log_sigmoid(f)
    logf = jnp.maximum(a, b) + jnp.log1p(jnp.exp(-jnp.abs(a - b)))
    key = (1.0 - lb) * _sigmoid(-f)

    g_hi, g_lo = _split_bf16(logf, 2)
    mexp = mexp_ref[...]
    expo = _dot(mexp, g_hi) + _dot(mexp, g_lo)

    lvl = lvl_ref[...]
    for h in range(heads):
        hs = slice(h * dk, (h + 1) * dk)
        qv = q_ref[:, hs]
        qh = qv * _sigmoid(qv)
        kh = key[:, hs]
        vh = i_ref[:, hs].astype(BF16)
        bh = expo[0:c, hs]
        st = st_ref[h]
        o = _dot_nt((qh * jnp.exp(bh)).astype(BF16), st.astype(BF16))
        att = jnp.where(lvl == n_lev, _dot_nt(qh.astype(BF16), kh.astype(BF16)), 0.0)
        for lev in range(n_lev):
            e = jnp.exp(expo[(1 + lev) * c:(2 + lev) * c, hs])
            att = jnp.where(lvl == lev, _dot_nt((qh * e).astype(BF16), (kh * e).astype(BF16)), att)
        o = o + _dot(att.astype(BF16), vh)
        khat = (kh * jnp.exp(expo[(n_lev + 1) * c:(n_lev + 2) * c, hs])).astype(BF16)
        st_ref[h] = st * jnp.exp(bh[c - 1:c, :]) + _dot_tn(vh, khat)
        ogv = og_ref[:, hs]
        o_ref[:, hs] = (_rms(o, gn_ref[...]) * (ogv * _sigmoid(ogv))).astype(o_ref.dtype)


def _hgrn2_call(q, f, i, og, lbraw, g_norm, layer):
    t, d = q.shape
    c = HG_CHUNK
    mexp, lvl, n_lev = _hgrn2_tables(c)
    dk = d // HG_HEADS
    blk = pl.BlockSpec((c, d), lambda n: (n, 0))
    return pl.pallas_call(
        functools.partial(_hgrn2_kernel, layer=layer, n_lev=n_lev, heads=HG_HEADS),
        grid=(t // c,),
        in_specs=[_resident(lbraw.shape), _resident(g_norm.shape), _resident(mexp.shape),
                  _resident(lvl.shape), blk, blk, blk, blk],
        out_specs=blk, out_shape=jax.ShapeDtypeStruct((t, d), BF16),
        scratch_shapes=[pltpu.VMEM((HG_HEADS, dk, dk), F32)],
        compiler_params=_params(("arbitrary",)), name="hgrn2",
    )(lbraw, g_norm, mexp, lvl, q, f, i, og)


def _mlstm_kernel(tril_ref, norm_ref, q_ref, k_ref, v_ref, og_ref, g_ref, gt_ref, o_ref,
                  c_ref, n_ref, m_ref, *, heads):
    c = q_ref.shape[0]
    dqk = q_ref.shape[1] // heads
    dv = v_ref.shape[1] // heads

    @pl.when(pl.program_id(0) == 0)
    def _():
        c_ref[...] = jnp.zeros_like(c_ref)
        n_ref[...] = jnp.zeros_like(n_ref)
        m_ref[...] = jnp.zeros_like(m_ref)

    tril = tril_ref[...]
    gates = g_ref[...]
    gates_t = gt_ref[...]
    cum_col = sum(_dot(tril, part) for part in _split_bf16(gates, 3))
    cum_row = sum(_dot_nt(part, tril) for part in _split_bf16(gates_t, 3))
    causal = lax.broadcasted_iota(jnp.int32, (c, c), 0) >= lax.broadcasted_iota(jnp.int32, (c, c), 1)

    for h in range(heads):
        b_col = cum_col[:, heads + h:heads + h + 1]
        b_row = cum_row[heads + h:heads + h + 1, :]
        i_col = gates[:, h:h + 1]
        i_row = gates_t[h:h + 1, :]
        m_prev = m_ref[h:h + 1, 0:1]
        qh = q_ref[:, h * dqk:(h + 1) * dqk]
        kh = k_ref[:, h * dqk:(h + 1) * dqk] * (dqk ** -0.5)
        vh = v_ref[:, h * dv:(h + 1) * dv].astype(BF16)
        cst = c_ref[h]
        nst = n_ref[h:h + 1, :]

        log_intra = jnp.where(causal, b_col - b_row + i_row, -jnp.inf)
        log_inter = b_col + m_prev
        m_t = jnp.maximum(jnp.max(log_intra, axis=1, keepdims=True), log_inter)
        w_intra = jnp.exp(log_intra - m_t)
        w_inter = jnp.exp(log_inter - m_t)
        qb = qh.astype(BF16)
        scores = _dot_nt(qb, kh.astype(BF16)) * w_intra
        num = w_inter * _dot(qb, cst.astype(BF16)) + _dot(scores.astype(BF16), vh)
        den = w_inter * jnp.sum(qh * nst, axis=1, keepdims=True) + jnp.sum(scores, axis=1, keepdims=True)
        hid = num / jnp.maximum(jnp.abs(den), jnp.exp(-m_t))

        b_last = b_col[c - 1:c, :]
        log_state = b_last + m_prev
        m_new = jnp.maximum(log_state, jnp.max(b_last - b_row + i_row, axis=1, keepdims=True))
        w_src = jnp.exp(b_last - b_col + i_col - m_new)
        decay = jnp.exp(log_state - m_new)
        khat = kh * w_src
        c_ref[h] = decay * cst + _dot_tn(khat.astype(BF16), vh)
        n_ref[h:h + 1, :] = decay * nst + jnp.sum(khat, axis=0, keepdims=True)
        m_ref[h:h + 1, :] = jnp.broadcast_to(m_new, (1, m_ref.shape[1]))

        ogv = og_ref[:, h * dv:(h + 1) * dv]
        o_ref[:, h * dv:(h + 1) * dv] = (
            _rms(hid, norm_ref[:, h * dv:(h + 1) * dv]) * _sigmoid(ogv)).astype(o_ref.dtype)


def _mlstm_call(q, k, v, og, gates, gates_t, norm):
    t = q.shape[0]
    d = v.shape[1]
    c = ML_CHUNK
    dqk = q.shape[1] // ML_HEADS
    dv = d // ML_HEADS
    tril = jnp.asarray(np.tril(np.ones((c, c), np.float32)), BF16)
    blk = lambda n: pl.BlockSpec((c, n), lambda s: (s, 0))
    return pl.pallas_call(
        functools.partial(_mlstm_kernel, heads=ML_HEADS),
        grid=(t // c,),
        in_specs=[_resident(tril.shape), _resident(norm.shape), blk(q.shape[1]), blk(k.shape[1]),
                  blk(d), blk(d), blk(LANES), pl.BlockSpec((2 * ML_HEADS, c), lambda s: (0, s))],
        out_specs=blk(d), out_shape=jax.ShapeDtypeStruct((t, d), BF16),
        scratch_shapes=[pltpu.VMEM((ML_HEADS, dqk, dv), F32), pltpu.VMEM((8, dqk), F32),
                        pltpu.VMEM((8, LANES), F32)],
        compiler_params=_params(("arbitrary",)), name="mlstm",
    )(tril, norm, q, k, v, og, gates, gates_t)


def _swa_bias(blk):
    group = SW_HEADS // SW_KV_HEADS
    kpos = np.arange(2 * blk)[:, None] - blk
    dist = np.arange(blk)[None, :] - kpos
    slopes = 2.0 ** (-8.0 * (np.arange(SW_HEADS) + 1) / SW_HEADS)
    ok = (dist >= 0) & (dist < SW_WINDOW)
    tables = []
    for first in (True, False):
        valid = ok & ((kpos >= 0) | (not first))
        bias = np.where(valid[None], -slopes[:, None, None] * dist[None], -np.inf)
        bias = bias.reshape(SW_KV_HEADS, group, 2 * blk, blk).transpose(0, 2, 1, 3)
        tables.append(bias.reshape(SW_KV_HEADS, 2 * blk, group * blk))
    return jnp.asarray(np.stack(tables), F32)


def _lane_mean_sq(x, n):
    ones = jnp.ones((x.shape[1], x.shape[1]), BF16)
    hi, lo = _split_bf16(x * x, 2)
    return (_dot(hi, ones) + _dot(lo, ones)) * (1.0 / n)


def _swa_kernel(qg_ref, kg_ref, sink_ref, bias_ref, q_ref, kp_ref, kc_ref, vp_ref, vc_ref, o_ref):
    blk = q_ref.shape[0]
    group = SW_HEADS // SW_KV_HEADS
    lo = lax.broadcasted_iota(jnp.int32, (1, LANES), 1) < SW_HD
    kk = jnp.concatenate([kp_ref[...], kc_ref[...]], axis=0)
    vv = jnp.concatenate([vp_ref[...], vc_ref[...]], axis=0)
    q_scale = qg_ref[...] * (SW_HD ** -0.5)
    for kg in range(SW_KV_HEADS // 2):
        kx = kk[:, kg * LANES:(kg + 1) * LANES]
        vx = vv[:, kg * LANES:(kg + 1) * LANES]
        kr = pltpu.roll(kx, SW_HD, axis=1)
        vr = pltpu.roll(vx, SW_HD, axis=1)
        for half in range(2):
            kv = 2 * kg + half
            k_dup = jnp.where(lo, kx, kr) if half == 0 else jnp.where(lo, kr, kx)
            kn = (k_dup * lax.rsqrt(_lane_mean_sq(k_dup, LANES) + EPS) * kg_ref[...]).astype(BF16)
            v_low = jnp.where(lo, vx if half == 0 else vr, 0.0).astype(BF16)
            v_high = jnp.where(lo, 0.0, vr if half == 0 else vx).astype(BF16)
            q_groups = range(kv * group // 2, (kv + 1) * group // 2)
            qs = []
            for j in q_groups:
                qx = q_ref[:, j * LANES:(j + 1) * LANES]
                for part in (jnp.where(lo, qx, 0.0), jnp.where(lo, 0.0, qx)):
                    qs.append((part * lax.rsqrt(_lane_mean_sq(part, SW_HD) + EPS) * q_scale).astype(BF16))
            s = _dot_nt(kn, jnp.concatenate(qs, axis=0)) + bias_ref[kv]
            sink = sink_ref[:, kv * group * blk:(kv + 1) * group * blk]
            mx = jnp.maximum(jnp.max(s, axis=0, keepdims=True), sink)
            p = jnp.exp(s - mx)
            inv = 1.0 / (jnp.sum(p, axis=0, keepdims=True) + jnp.exp(sink - mx))
            pn = (p * inv).astype(BF16)
            for idx, j in enumerate(q_groups):
                even = pn[:, 2 * idx * blk:(2 * idx + 1) * blk]
                odd = pn[:, (2 * idx + 1) * blk:(2 * idx + 2) * blk]
                o_ref[:, j * LANES:(j + 1) * LANES] = (_dot_tn(even, v_low) + _dot_tn(odd, v_high)).astype(o_ref.dtype)


def _swa_call(q, k, v, q_gain, k_gain, sinks):
    t, nq = q.shape
    nk = k.shape[1]
    b = SW_BLOCK
    bias = _swa_bias(b)
    qg2 = jnp.tile(q_gain, (1, LANES // SW_HD))
    kg2 = jnp.tile(k_gain, (1, LANES // SW_HD))
    sink_row = jnp.repeat(sinks, b, axis=1)
    cur = lambda n: pl.BlockSpec((b, n), lambda s: (s, 0))
    prev = lambda n: pl.BlockSpec((b, n), lambda s: (jnp.maximum(s - 1, 0), 0))
    bias_spec = pl.BlockSpec((None,) + bias.shape[1:], lambda s: (jnp.minimum(s, 1), 0, 0, 0))
    return pl.pallas_call(
        _swa_kernel, grid=(t // b,),
        in_specs=[_resident(qg2.shape), _resident(kg2.shape), _resident(sink_row.shape), bias_spec,
                  cur(nq), prev(nk), cur(nk), prev(nk), cur(nk)],
        out_specs=cur(nq), out_shape=jax.ShapeDtypeStruct((t, nq), BF16),
        compiler_params=_params(("arbitrary",)), name="swa",
    )(qg2, kg2, sink_row, bias, q, k, k, v, v)


def kernel(x, p, norm_gains, w_ffn_gu, w_ffn_down, w_ple_gate, w_ple_proj, hg_lower_bounds, hg_w_in,
           hg_g_norm, hg_w_out, ml_w_qkvo, ml_w_if, ml_b_if, ml_norm, ml_w_out, sw_w_qkv, sw_q_norm,
           sw_k_norm, sw_sinks, sw_w_o):
    batch, seq, d = x.shape
    depth = p.shape[0]
    outs = []
    for bi in range(batch):
        xs = x[bi]
        for layer in range(depth):
            kind, j = layer % N_MIXERS, layer // N_MIXERS
            gains = norm_gains[layer]
            wgu = w_ffn_gu[layer].astype(BF16)
            wd = w_ffn_down[layer].astype(BF16)
            if kind == 0:
                h, q, f, i, og = _pre_call(xs, gains, wgu[0], wd[0], hg_w_in[j].astype(BF16),
                                           (d, d, d, d), (F32, F32, F32, F32))
                mix = _hgrn2_call(q, f, i, og, hg_lower_bounds, hg_g_norm[j][None, :], layer)
                wout = hg_w_out[j]
            elif kind == 1:
                dqk = (ml_w_qkvo.shape[2] - 2 * d) // 2
                h, q, k, v, og, gates, gates_t = _pre_call(
                    xs, gains, wgu[0], wd[0], ml_w_qkvo[j].astype(BF16), (dqk, dqk, d, d),
                    (F32, F32, F32, F32), gate_w=ml_w_if[j], gate_b=ml_b_if[j])
                mix = _mlstm_call(q, k, v, og, gates, gates_t, ml_norm[j][None, :])
                wout = ml_w_out[j]
            else:
                nq = SW_HEADS * SW_HD
                nk = SW_KV_HEADS * SW_HD
                h, q, k, v = _pre_call(xs, gains, wgu[0], wd[0], sw_w_qkv[j].astype(BF16),
                                       (nq, nk, nk), (F32, F32, F32))
                mix = _swa_call(q, k, v, sw_q_norm[j][None, :], sw_k_norm[j][None, :], sw_sinks[j][None, :])
                wout = sw_w_o[j]
            xs = _post_call(h, mix, p[layer, bi], gains, wout.astype(BF16), wgu[1], wd[1],
                            w_ple_gate[layer].astype(BF16), w_ple_proj[layer].astype(BF16))
        outs.append(xs)
    return jnp.stack(outs, axis=0)
```

```python
import functools

import numpy as np
import jax
import jax.numpy as jnp
from jax import lax
from jax.experimental import pallas as pl
from jax.experimental.pallas import tpu as pltpu

F32 = jnp.float32
BF16 = jnp.bfloat16

EPS = 1e-6
N_MIXERS = 3
HG_HEADS = 8
ML_HEADS = 4
ML_GATE_CAP = 15.0
SW_HEADS = 16
SW_KV_HEADS = 4
SW_HD = 64
SW_WINDOW = 128

V7X_VMEM_LIMIT_BYTES = 56 * 1024 * 1024
LANES = 128

ROW_TILE = 512
FF_CHUNK = 256
HG_CHUNK = 128
ML_CHUNK = 256
SW_BLOCK = 128


def _dot(a, b):
    return jnp.dot(a, b, preferred_element_type=F32)


def _dot_nt(a, b):
    return lax.dot_general(a, b, (((1,), (1,)), ((), ())), preferred_element_type=F32)


def _dot_tn(a, b):
    return lax.dot_general(a, b, (((0,), (0,)), ((), ())), preferred_element_type=F32)


def _sigmoid(x):
    return 0.5 * jnp.tanh(0.5 * x) + 0.5


def _log_sigmoid(x):
    return jnp.minimum(x, 0.0) - jnp.log1p(jnp.exp(-jnp.abs(x)))


def _rms(x, gain):
    return x * lax.rsqrt(jnp.mean(x * x, axis=-1, keepdims=True) + EPS) * gain


def _split_bf16(x, terms):
    parts = []
    r = x
    for _ in range(terms):
        p = r.astype(BF16)
        parts.append(p)
        r = r - p.astype(F32)
    return parts


def _swiglu_residual(x, gain, wgu_ref, wd_ref):
    d_ff = wd_ref.shape[0]
    xn = _rms(x, gain).astype(BF16)
    y = jnp.zeros_like(x)
    for j in range(d_ff // FF_CHUNK):
        lo = j * FF_CHUNK
        g = _dot(xn, wgu_ref[:, lo:lo + FF_CHUNK])
        u = _dot(xn, wgu_ref[:, d_ff + lo:d_ff + lo + FF_CHUNK])
        a = (g * _sigmoid(g) * u).astype(BF16)
        y = y + _dot(a, wd_ref[lo:lo + FF_CHUNK, :])
    return x + 0.5 * y


def _resident(shape, lead=()):
    nd = len(shape)
    block = (None,) * len(lead) + tuple(shape[len(lead):])
    index = tuple(lead) + (0,) * (nd - len(lead))
    return pl.BlockSpec(block, lambda *_: index, pipeline_mode=pl.Buffered(1))


def _params(semantics):
    return pltpu.CompilerParams(dimension_semantics=semantics,
                                vmem_limit_bytes=V7X_VMEM_LIMIT_BYTES)


def _pre_kernel(*refs, splits, n_gate):
    if n_gate:
        (x_ref, gains_ref, wgu_ref, wd_ref, win_ref, wif_ref, bif_ref), outs = refs[:7], refs[7:]
    else:
        (x_ref, gains_ref, wgu_ref, wd_ref, win_ref), outs = refs[:5], refs[5:]
    h_ref = outs[0]
    proj_refs = outs[1:1 + len(splits)]
    h = _swiglu_residual(x_ref[...], gains_ref[0:1, :], wgu_ref, wd_ref)
    h_ref[...] = h
    xn = _rms(h, gains_ref[1:2, :])
    xb = xn.astype(BF16)
    off = 0
    for r, n in zip(proj_refs, splits):
        r[...] = _dot(xb, win_ref[:, off:off + n]).astype(r.dtype)
        off += n
    if n_gate:
        g_ref, gt_ref = outs[1 + len(splits):]
        xh, xl = _split_bf16(xn, 2)
        wh, wl = _split_bf16(wif_ref[...], 2)
        pre = _dot(xh, wh) + _dot(xl, wh) + _dot(xh, wl) + bif_ref[...]
        capped = ML_GATE_CAP * jnp.tanh(pre / ML_GATE_CAP)
        lane = lax.broadcasted_iota(jnp.int32, capped.shape, 1)
        gates = jnp.where(lane < n_gate // 2, capped, _log_sigmoid(capped))
        g_ref[...] = gates
        gt_ref[...] = gates.T[0:gt_ref.shape[0], :]


def _pre_call(x, bi, gains, layer, wgu, wd, win, j, splits, out_dtypes, gate_w=None, gate_b=None):
    _, t, d = x.shape
    n_gate = 0 if gate_w is None else gate_w.shape[1]
    tm = ROW_TILE
    row = lambda n: pl.BlockSpec((tm, n), lambda i: (i, 0))
    in_specs = [pl.BlockSpec((None, tm, d), lambda i: (bi, i, 0)), _resident(gains.shape, (layer,)),
                _resident(wgu.shape, (layer, 0)), _resident(wd.shape, (layer, 0)), _resident(win.shape, (j,))]
    args = [x, gains, wgu, wd, win]
    out_shape = [jax.ShapeDtypeStruct((t, d), F32)]
    out_specs = [row(d)]
    for n, dt in zip(splits, out_dtypes):
        out_shape.append(jax.ShapeDtypeStruct((t, n), dt))
        out_specs.append(row(n))
    if n_gate:
        wpad = jnp.zeros((d, LANES), F32).at[:, :n_gate].set(gate_w)
        bpad = jnp.zeros((1, LANES), F32).at[0, :n_gate].set(gate_b)
        in_specs += [_resident(wpad.shape), _resident(bpad.shape)]
        args += [wpad, bpad]
        out_shape += [jax.ShapeDtypeStruct((t, LANES), F32), jax.ShapeDtypeStruct((n_gate, t), F32)]
        out_specs += [row(LANES), pl.BlockSpec((n_gate, tm), lambda i: (0, i))]
    return pl.pallas_call(
        functools.partial(_pre_kernel, splits=tuple(splits), n_gate=n_gate),
        grid=(t // tm,), in_specs=in_specs, out_specs=out_specs, out_shape=out_shape,
        compiler_params=_params(("parallel",)), name="pre",
    )(*args)


def _post_kernel(h_ref, o_ref, p_ref, gains_ref, wout_ref, wgu_ref, wd_ref, wpg_ref, wpp_ref, x_ref):
    h = h_ref[...] + _dot(o_ref[...], wout_ref[...])
    h = _swiglu_residual(h, gains_ref[2:3, :], wgu_ref, wd_ref)
    gate = _sigmoid(_dot(_rms(h, gains_ref[3:4, :]).astype(BF16), wpg_ref[...]))
    x_ref[...] = h + gate * _dot(p_ref[...].astype(BF16), wpp_ref[...])


def _post_call(h, o, p, bi, gains, layer, wout, j, wgu, wd, wpg, wpp):
    t, d = h.shape
    tm = ROW_TILE
    row = lambda n: pl.BlockSpec((tm, n), lambda i: (i, 0))
    return pl.pallas_call(
        _post_kernel, grid=(t // tm,),
        in_specs=[row(d), row(o.shape[1]), pl.BlockSpec((None, None, tm, p.shape[3]), lambda i: (layer, bi, i, 0)),
                  _resident(gains.shape, (layer,)), _resident(wout.shape, (j,)), _resident(wgu.shape, (layer, 1)),
                  _resident(wd.shape, (layer, 1)), _resident(wpg.shape, (layer,)), _resident(wpp.shape, (layer,))],
        out_specs=pl.BlockSpec((None, tm, d), lambda i: (0, i, 0)),
        out_shape=jax.ShapeDtypeStruct((1, t, d), F32),
        compiler_params=_params(("parallel",)), name="post",
    )(h, o, p, gains, wout, wgu, wd, wpg, wpp)


def _hgrn2_tables(c):
    n_lev = int(np.log2(c))
    assert 2 ** n_lev == c
    mats = np.zeros((n_lev + 2, c, c), np.float32)
    r = np.arange(c)[:, None]
    j = np.arange(c)[None, :]
    mats[0] = j <= r
    for lev in range(n_lev):
        half = 2 ** lev
        m = (r // (2 * half)) * (2 * half) + half
        mats[1 + lev] = np.where(r >= m, (j >= m) & (j <= r), (j > r) & (j <= m - 1))
    mats[n_lev + 1] = j > r
    x = r ^ j
    lvl = np.where(r > j, np.floor(np.log2(np.maximum(x, 1))).astype(np.int32), -1)
    lvl = np.where(r == j, n_lev, lvl).astype(np.int32)
    mats = mats.reshape(-1, c)
    return jnp.asarray(np.concatenate([mats, mats], axis=1), BF16), jnp.asarray(lvl), n_lev


def _hgrn2_kernel(lbraw_ref, gn_ref, mexp_ref, lvl_ref, q_ref, f_ref, i_ref, og_ref, o_ref, st_ref,
                  *, layer, n_lev, heads):
    c, d = q_ref.shape
    dk = d // heads

    @pl.when(pl.program_id(0) == 0)
    def _():
        st_ref[...] = jnp.zeros_like(st_ref)

    rows = [lbraw_ref[r:r + 1, :] for r in range(lbraw_ref.shape[0])]
    mx = functools.reduce(jnp.maximum, rows)
    es = [jnp.exp(r - mx) for r in rows]
    lb = sum(es[1:layer + 1], jnp.zeros_like(mx)) / sum(es)

    f = f_ref[...]
    t = jnp.exp(-jnp.abs(f))
    r = 1.0 / (1.0 + t)
    pos = f >= 0.0
    tiny = jnp.logical_and(lb <= 0.0, jnp.logical_not(pos))
    num = jnp.where(pos, 1.0 + lb * t, lb + t)
    logf = jnp.log(jnp.where(tiny, 1.0, num) * r) + jnp.where(tiny, f, 0.0)
    key = (1.0 - lb) * jnp.where(pos, t * r, r)

    expo = _dot(mexp_ref[...], jnp.concatenate(_split_bf16(logf, 2), axis=0))

    lvl = lvl_ref[...]
    row = lax.broadcasted_iota(jnp.int32, (c, 1), 0)
    for h in range(heads):
        hs = slice(h * dk, (h + 1) * dk)
        qv = q_ref[:, hs]
        qh = qv * _sigmoid(qv)
        kh = key[:, hs]
        vh = i_ref[:, hs].astype(BF16)
        bh = expo[0:c, hs]
        st = st_ref[h]
        o = _dot_nt((qh * jnp.exp(bh)).astype(BF16), st.astype(BF16))
        att = jnp.where(lvl == n_lev, _dot_nt(qh.astype(BF16), kh.astype(BF16)), 0.0)
        for lev in range(n_lev):
            upper = (row & (1 << lev)) != 0
            x = (jnp.where(upper, qh, kh) * jnp.exp(expo[(1 + lev) * c:(2 + lev) * c, hs])).astype(BF16)
            att = jnp.where(lvl == lev, _dot_nt(x, x), att)
        o = o + _dot(att.astype(BF16), vh)
        khat = (kh * jnp.exp(expo[(n_lev + 1) * c:(n_lev + 2) * c, hs])).astype(BF16)
        st_ref[h] = st * jnp.exp(bh[c - 1:c, :]) + _dot_tn(vh, khat)
        ogv = og_ref[:, hs]
        o_ref[:, hs] = (_rms(o, gn_ref[...]) * (ogv * _sigmoid(ogv))).astype(o_ref.dtype)


def _hgrn2_call(q, f, i, og, lbraw, g_norm, layer):
    t, d = q.shape
    c = HG_CHUNK
    mexp, lvl, n_lev = _hgrn2_tables(c)
    dk = d // HG_HEADS
    blk = pl.BlockSpec((c, d), lambda n: (n, 0))
    return pl.pallas_call(
        functools.partial(_hgrn2_kernel, layer=layer, n_lev=n_lev, heads=HG_HEADS),
        grid=(t // c,),
        in_specs=[_resident(lbraw.shape), _resident(g_norm.shape), _resident(mexp.shape),
                  _resident(lvl.shape), blk, blk, blk, blk],
        out_specs=blk, out_shape=jax.ShapeDtypeStruct((t, d), BF16),
        scratch_shapes=[pltpu.VMEM((HG_HEADS, dk, dk), F32)],
        compiler_params=_params(("arbitrary",)), name="hgrn2",
    )(lbraw, g_norm, mexp, lvl, q, f, i, og)


def _mlstm_kernel(tril_ref, norm_ref, q_ref, k_ref, v_ref, og_ref, g_ref, gt_ref, o_ref,
                  c_ref, n_ref, m_ref, *, heads):
    c = q_ref.shape[0]
    dqk = q_ref.shape[1] // heads
    dv = v_ref.shape[1] // heads

    @pl.when(pl.program_id(0) == 0)
    def _():
        c_ref[...] = jnp.zeros_like(c_ref)
        n_ref[...] = jnp.zeros_like(n_ref)
        m_ref[...] = jnp.zeros_like(m_ref)

    tril = tril_ref[...]
    gates = g_ref[...]
    gates_t = gt_ref[...]
    cum_col = sum(_dot(tril, part) for part in _split_bf16(gates, 3))
    cum_row = sum(_dot_nt(part, tril) for part in _split_bf16(gates_t, 3))
    causal = lax.broadcasted_iota(jnp.int32, (c, c), 0) >= lax.broadcasted_iota(jnp.int32, (c, c), 1)

    for h in range(heads):
        b_col = cum_col[:, heads + h:heads + h + 1]
        b_row = cum_row[heads + h:heads + h + 1, :]
        i_col = gates[:, h:h + 1]
        i_row = gates_t[h:h + 1, :]
        m_prev = m_ref[h:h + 1, 0:1]
        qh = q_ref[:, h * dqk:(h + 1) * dqk]
        kh = k_ref[:, h * dqk:(h + 1) * dqk] * (dqk ** -0.5)
        vh = v_ref[:, h * dv:(h + 1) * dv].astype(BF16)
        cst = c_ref[h]
        nst = n_ref[h:h + 1, :]

        log_intra = jnp.where(causal, b_col - b_row + i_row, -jnp.inf)
        log_inter = b_col + m_prev
        m_t = jnp.maximum(jnp.max(log_intra, axis=1, keepdims=True), log_inter)
        w_intra = jnp.exp(log_intra - m_t)
        w_inter = jnp.exp(log_inter - m_t)
        qb = qh.astype(BF16)
        scores = _dot_nt(qb, kh.astype(BF16)) * w_intra
        num = w_inter * _dot(qb, cst.astype(BF16)) + _dot(scores.astype(BF16), vh)
        den = w_inter * jnp.sum(qh * nst, axis=1, keepdims=True) + jnp.sum(scores, axis=1, keepdims=True)
        hid = num / jnp.maximum(jnp.abs(den), jnp.exp(-m_t))

        b_last = b_col[c - 1:c, :]
        log_state = b_last + m_prev
        m_new = jnp.maximum(log_state, jnp.max(b_last - b_row + i_row, axis=1, keepdims=True))
        w_src = jnp.exp(b_last - b_col + i_col - m_new)
        decay = jnp.exp(log_state - m_new)
        khat = kh * w_src
        c_ref[h] = decay * cst + _dot_tn(khat.astype(BF16), vh)
        n_ref[h:h + 1, :] = decay * nst + jnp.sum(khat, axis=0, keepdims=True)
        m_ref[h:h + 1, :] = jnp.broadcast_to(m_new, (1, m_ref.shape[1]))

        ogv = og_ref[:, h * dv:(h + 1) * dv]
        o_ref[:, h * dv:(h + 1) * dv] = (
            _rms(hid, norm_ref[:, h * dv:(h + 1) * dv]) * _sigmoid(ogv)).astype(o_ref.dtype)


def _mlstm_call(q, k, v, og, gates, gates_t, norm):
    t = q.shape[0]
    d = v.shape[1]
    c = ML_CHUNK
    dqk = q.shape[1] // ML_HEADS
    dv = d // ML_HEADS
    tril = jnp.asarray(np.tril(np.ones((c, c), np.float32)), BF16)
    blk = lambda n: pl.BlockSpec((c, n), lambda s: (s, 0))
    return pl.pallas_call(
        functools.partial(_mlstm_kernel, heads=ML_HEADS),
        grid=(t // c,),
        in_specs=[_resident(tril.shape), _resident(norm.shape), blk(q.shape[1]), blk(k.shape[1]),
                  blk(d), blk(d), blk(LANES), pl.BlockSpec((2 * ML_HEADS, c), lambda s: (0, s))],
        out_specs=blk(d), out_shape=jax.ShapeDtypeStruct((t, d), BF16),
        scratch_shapes=[pltpu.VMEM((ML_HEADS, dqk, dv), F32), pltpu.VMEM((8, dqk), F32),
                        pltpu.VMEM((8, LANES), F32)],
        compiler_params=_params(("arbitrary",)), name="mlstm",
    )(tril, norm, q, k, v, og, gates, gates_t)


def _swa_bias(blk):
    group = SW_HEADS // SW_KV_HEADS
    kpos = np.arange(2 * blk)[:, None] - blk
    dist = np.arange(blk)[None, :] - kpos
    slopes = 2.0 ** (-8.0 * (np.arange(SW_HEADS) + 1) / SW_HEADS)
    ok = (dist >= 0) & (dist < SW_WINDOW)
    tables = []
    for first in (True, False):
        valid = ok & ((kpos >= 0) | (not first))
        bias = np.where(valid[None], -slopes[:, None, None] * dist[None], -np.inf)
        bias = bias.reshape(SW_KV_HEADS, group, 2 * blk, blk).transpose(0, 2, 1, 3)
        tables.append(bias.reshape(SW_KV_HEADS, 2 * blk, group * blk))
    return jnp.asarray(np.stack(tables), F32)


def _lane_mean_sq(x, n):
    ones = jnp.ones((x.shape[1], x.shape[1]), BF16)
    hi, lo = _split_bf16(x * x, 2)
    return (_dot(hi, ones) + _dot(lo, ones)) * (1.0 / n)


def _swa_kernel(qg_ref, kg_ref, sink_ref, bias_ref, q_ref, kp_ref, kc_ref, vp_ref, vc_ref, o_ref):
    blk = q_ref.shape[0]
    group = SW_HEADS // SW_KV_HEADS
    lo = lax.broadcasted_iota(jnp.int32, (1, LANES), 1) < SW_HD
    kk = jnp.concatenate([kp_ref[...], kc_ref[...]], axis=0)
    vv = jnp.concatenate([vp_ref[...], vc_ref[...]], axis=0)
    q_scale = qg_ref[...] * (SW_HD ** -0.5)
    for kg in range(SW_KV_HEADS // 2):
        kx = kk[:, kg * LANES:(kg + 1) * LANES]
        vx = vv[:, kg * LANES:(kg + 1) * LANES]
        kr = pltpu.roll(kx, SW_HD, axis=1)
        vr = pltpu.roll(vx, SW_HD, axis=1)
        for half in range(2):
            kv = 2 * kg + half
            k_dup = jnp.where(lo, kx, kr) if half == 0 else jnp.where(lo, kr, kx)
            kn = (k_dup * lax.rsqrt(_lane_mean_sq(k_dup, LANES) + EPS) * kg_ref[...]).astype(BF16)
            v_low = jnp.where(lo, vx if half == 0 else vr, 0.0).astype(BF16)
            v_high = jnp.where(lo, 0.0, vr if half == 0 else vx).astype(BF16)
            q_groups = range(kv * group // 2, (kv + 1) * group // 2)
            qs = []
            for j in q_groups:
                qx = q_ref[:, j * LANES:(j + 1) * LANES]
                for part in (jnp.where(lo, qx, 0.0), jnp.where(lo, 0.0, qx)):
                    qs.append((part * lax.rsqrt(_lane_mean_sq(part, SW_HD) + EPS) * q_scale).astype(BF16))
            s = _dot_nt(kn, jnp.concatenate(qs, axis=0)) + bias_ref[kv]
            sink = sink_ref[:, kv * group * blk:(kv + 1) * group * blk]
            mx = jnp.maximum(jnp.max(s, axis=0, keepdims=True), sink)
            p = jnp.exp(s - mx)
            inv = 1.0 / (jnp.sum(p, axis=0, keepdims=True) + jnp.exp(sink - mx))
            pn = (p * inv).astype(BF16)
            for idx, j in enumerate(q_groups):
                even = pn[:, 2 * idx * blk:(2 * idx + 1) * blk]
                odd = pn[:, (2 * idx + 1) * blk:(2 * idx + 2) * blk]
                o_ref[:, j * LANES:(j + 1) * LANES] = (_dot_tn(even, v_low) + _dot_tn(odd, v_high)).astype(o_ref.dtype)


def _swa_call(q, k, v, q_gain, k_gain, sinks):
    t, nq = q.shape
    nk = k.shape[1]
    b = SW_BLOCK
    bias = _swa_bias(b)
    qg2 = jnp.tile(q_gain, (1, LANES // SW_HD))
    kg2 = jnp.tile(k_gain, (1, LANES // SW_HD))
    sink_row = jnp.repeat(sinks, b, axis=1)
    cur = lambda n: pl.BlockSpec((b, n), lambda s: (s, 0))
    prev = lambda n: pl.BlockSpec((b, n), lambda s: (jnp.maximum(s - 1, 0), 0))
    bias_spec = pl.BlockSpec((None,) + bias.shape[1:], lambda s: (jnp.minimum(s, 1), 0, 0, 0))
    return pl.pallas_call(
        _swa_kernel, grid=(t // b,),
        in_specs=[_resident(qg2.shape), _resident(kg2.shape), _resident(sink_row.shape), bias_spec,
                  cur(nq), prev(nk), cur(nk), prev(nk), cur(nk)],
        out_specs=cur(nq), out_shape=jax.ShapeDtypeStruct((t, nq), BF16),
        compiler_params=_params(("arbitrary",)), name="swa",
    )(qg2, kg2, sink_row, bias, q, k, k, v, v)


def kernel(x, p, norm_gains, w_ffn_gu, w_ffn_down, w_ple_gate, w_ple_proj, hg_lower_bounds, hg_w_in,
           hg_g_norm, hg_w_out, ml_w_qkvo, ml_w_if, ml_b_if, ml_norm, ml_w_out, sw_w_qkv, sw_q_norm,
           sw_k_norm, sw_sinks, sw_w_o):
    batch, seq, d = x.shape
    depth = p.shape[0]
    wgu, wd = w_ffn_gu.astype(BF16), w_ffn_down.astype(BF16)
    wpg, wpp = w_ple_gate.astype(BF16), w_ple_proj.astype(BF16)
    w_in = (hg_w_in.astype(BF16), ml_w_qkvo.astype(BF16), sw_w_qkv.astype(BF16))
    w_out = (hg_w_out.astype(BF16), ml_w_out.astype(BF16), sw_w_o.astype(BF16))
    outs = []
    for bi in range(batch):
        xs, xb = x, bi
        for layer in range(depth):
            kind, j = layer % N_MIXERS, layer // N_MIXERS
            pre = functools.partial(_pre_call, xs, xb, norm_gains, layer, wgu, wd, w_in[kind], j)
            if kind == 0:
                h, q, f, i, og = pre((d, d, d, d), (F32, F32, F32, F32))
                mix = _hgrn2_call(q, f, i, og, hg_lower_bounds, hg_g_norm[j][None, :], layer)
            elif kind == 1:
                dqk = (ml_w_qkvo.shape[2] - 2 * d) // 2
                h, q, k, v, og, gates, gates_t = pre((dqk, dqk, d, d), (F32, F32, F32, F32),
                                                     gate_w=ml_w_if[j], gate_b=ml_b_if[j])
                mix = _mlstm_call(q, k, v, og, gates, gates_t, ml_norm[j][None, :])
            else:
                nq = SW_HEADS * SW_HD
                nk = SW_KV_HEADS * SW_HD
                h, q, k, v = pre((nq, nk, nk), (F32, F32, F32))
                mix = _swa_call(q, k, v, sw_q_norm[j][None, :], sw_k_norm[j][None, :], sw_sinks[j][None, :])
            xs = _post_call(h, mix, p, bi, norm_gains, layer, w_out[kind], j, wgu, wd, wpg, wpp)
            xb = 0
        outs.append(xs)
    return outs[0] if batch == 1 else jnp.concatenate(outs, axis=0)
```

```python
import functools

import numpy as np
import jax
import jax.numpy as jnp
from jax import lax
from jax.experimental import pallas as pl
from jax.experimental.pallas import tpu as pltpu

F32 = jnp.float32
BF16 = jnp.bfloat16

EPS = 1e-6
N_MIXERS = 3
HG_HEADS = 8
ML_HEADS = 4
ML_GATE_CAP = 15.0
SW_HEADS = 16
SW_KV_HEADS = 4
SW_HD = 64
SW_WINDOW = 128

V7X_VMEM_LIMIT_BYTES = 56 * 1024 * 1024
LANES = 128

ROW_TILE = 512
FF_CHUNK = 256
HG_CHUNK = 128
ML_CHUNK = 256
SW_BLOCK = 128


def _dot(a, b):
    return jnp.dot(a, b, preferred_element_type=F32)


def _dot_nt(a, b):
    return lax.dot_general(a, b, (((1,), (1,)), ((), ())), preferred_element_type=F32)


def _dot_tn(a, b):
    return lax.dot_general(a, b, (((0,), (0,)), ((), ())), preferred_element_type=F32)


def _sigmoid(x):
    return 0.5 * jnp.tanh(0.5 * x) + 0.5


def _log_sigmoid(x):
    return jnp.minimum(x, 0.0) - jnp.log1p(jnp.exp(-jnp.abs(x)))


def _rms(x, gain):
    return x * lax.rsqrt(jnp.mean(x * x, axis=-1, keepdims=True) + EPS) * gain


def _split_bf16(x, terms):
    parts = []
    r = x
    for _ in range(terms):
        p = r.astype(BF16)
        parts.append(p)
        r = r - p.astype(F32)
    return parts


def _swiglu_residual(x, gain, wgu_ref, wd_ref):
    d_ff = wd_ref.shape[0]
    xn = _rms(x, gain).astype(BF16)
    y = jnp.zeros_like(x)
    for j in range(d_ff // FF_CHUNK):
        lo = j * FF_CHUNK
        g = _dot(xn, wgu_ref[:, lo:lo + FF_CHUNK])
        u = _dot(xn, wgu_ref[:, d_ff + lo:d_ff + lo + FF_CHUNK])
        a = (g * _sigmoid(g) * u).astype(BF16)
        y = y + _dot(a, wd_ref[lo:lo + FF_CHUNK, :])
    return x + 0.5 * y


def _resident(shape, lead=()):
    nd = len(shape)
    block = (None,) * len(lead) + tuple(shape[len(lead):])
    index = tuple(lead) + (0,) * (nd - len(lead))
    return pl.BlockSpec(block, lambda *_: index, pipeline_mode=pl.Buffered(1))


def _params(semantics):
    return pltpu.CompilerParams(dimension_semantics=semantics,
                                vmem_limit_bytes=V7X_VMEM_LIMIT_BYTES)


def _pre_kernel(*refs, splits, n_gate):
    if n_gate:
        (x_ref, gains_ref, wgu_ref, wd_ref, win_ref, wif_ref, bif_ref), outs = refs[:7], refs[7:]
    else:
        (x_ref, gains_ref, wgu_ref, wd_ref, win_ref), outs = refs[:5], refs[5:]
    h_ref = outs[0]
    proj_refs = outs[1:1 + len(splits)]
    h = _swiglu_residual(x_ref[...], gains_ref[0:1, :], wgu_ref, wd_ref)
    h_ref[...] = h
    xn = _rms(h, gains_ref[1:2, :])
    xb = xn.astype(BF16)
    off = 0
    for r, n in zip(proj_refs, splits):
        r[...] = _dot(xb, win_ref[:, off:off + n]).astype(r.dtype)
        off += n
    if n_gate:
        g_ref, gt_ref = outs[1 + len(splits):]
        xh, xl = _split_bf16(xn, 2)
        wh, wl = _split_bf16(wif_ref[...], 2)
        pre = _dot(xh, wh) + _dot(xl, wh) + _dot(xh, wl) + bif_ref[...]
        capped = ML_GATE_CAP * jnp.tanh(pre / ML_GATE_CAP)
        lane = lax.broadcasted_iota(jnp.int32, capped.shape, 1)
        gates = jnp.where(lane < n_gate // 2, capped, _log_sigmoid(capped))
        g_ref[...] = gates
        gt_ref[...] = gates.T[0:gt_ref.shape[0], :]


def _pre_call(x, bi, gains, layer, wgu, wd, win, j, splits, out_dtypes, gate_w=None, gate_b=None):
    _, t, d = x.shape
    n_gate = 0 if gate_w is None else gate_w.shape[1]
    tm = ROW_TILE
    row = lambda n: pl.BlockSpec((tm, n), lambda i: (i, 0))
    in_specs = [pl.BlockSpec((None, tm, d), lambda i: (bi, i, 0)), _resident(gains.shape, (layer,)),
                _resident(wgu.shape, (layer, 0)), _resident(wd.shape, (layer, 0)), _resident(win.shape, (j,))]
    args = [x, gains, wgu, wd, win]
    out_shape = [jax.ShapeDtypeStruct((t, d), F32)]
    out_specs = [row(d)]
    for n, dt in zip(splits, out_dtypes):
        out_shape.append(jax.ShapeDtypeStruct((t, n), dt))
        out_specs.append(row(n))
    if n_gate:
        wpad = jnp.zeros((d, LANES), F32).at[:, :n_gate].set(gate_w)
        bpad = jnp.zeros((1, LANES), F32).at[0, :n_gate].set(gate_b)
        in_specs += [_resident(wpad.shape), _resident(bpad.shape)]
        args += [wpad, bpad]
        out_shape += [jax.ShapeDtypeStruct((t, LANES), F32), jax.ShapeDtypeStruct((n_gate, t), F32)]
        out_specs += [row(LANES), pl.BlockSpec((n_gate, tm), lambda i: (0, i))]
    return pl.pallas_call(
        functools.partial(_pre_kernel, splits=tuple(splits), n_gate=n_gate),
        grid=(t // tm,), in_specs=in_specs, out_specs=out_specs, out_shape=out_shape,
        compiler_params=_params(("parallel",)), name="pre",
    )(*args)


def _post_kernel(h_ref, o_ref, p_ref, gains_ref, wout_ref, wgu_ref, wd_ref, wpg_ref, wpp_ref, x_ref):
    h = h_ref[...] + _dot(o_ref[...], wout_ref[...])
    h = _swiglu_residual(h, gains_ref[2:3, :], wgu_ref, wd_ref)
    gate = _sigmoid(_dot(_rms(h, gains_ref[3:4, :]).astype(BF16), wpg_ref[...]))
    x_ref[...] = h + gate * _dot(p_ref[...].astype(BF16), wpp_ref[...])


def _post_call(h, o, p, bi, gains, layer, wout, j, wgu, wd, wpg, wpp):
    t, d = h.shape
    tm = ROW_TILE
    row = lambda n: pl.BlockSpec((tm, n), lambda i: (i, 0))
    return pl.pallas_call(
        _post_kernel, grid=(t // tm,),
        in_specs=[row(d), row(o.shape[1]), pl.BlockSpec((None, None, tm, p.shape[3]), lambda i: (layer, bi, i, 0)),
                  _resident(gains.shape, (layer,)), _resident(wout.shape, (j,)), _resident(wgu.shape, (layer, 1)),
                  _resident(wd.shape, (layer, 1)), _resident(wpg.shape, (layer,)), _resident(wpp.shape, (layer,))],
        out_specs=pl.BlockSpec((None, tm, d), lambda i: (0, i, 0)),
        out_shape=jax.ShapeDtypeStruct((1, t, d), F32),
        compiler_params=_params(("parallel",)), name="post",
    )(h, o, p, gains, wout, wgu, wd, wpg, wpp)


def _hgrn2_tables(c):
    n_lev = int(np.log2(c))
    assert 2 ** n_lev == c
    mats = np.zeros((n_lev + 2, c, c), np.float32)
    r = np.arange(c)[:, None]
    j = np.arange(c)[None, :]
    mats[0] = j <= r
    for lev in range(n_lev):
        half = 2 ** lev
        m = (r // (2 * half)) * (2 * half) + half
        mats[1 + lev] = np.where(r >= m, (j >= m) & (j <= r), (j > r) & (j <= m - 1))
    mats[n_lev + 1] = j > r
    x = r ^ j
    lvl = np.where(r > j, np.floor(np.log2(np.maximum(x, 1))).astype(np.int32), -1)
    lvl = np.where(r == j, n_lev, lvl).astype(np.int32)
    mats = mats.reshape(-1, c)
    return jnp.asarray(np.concatenate([mats, mats], axis=1), BF16), jnp.asarray(lvl), n_lev


def _hgrn2_kernel(lbraw_ref, gn_ref, mexp_ref, lvl_ref, q_ref, f_ref, i_ref, og_ref, o_ref, st_ref,
                  *, layer, n_lev, heads):
    c, d = q_ref.shape
    dk = d // heads

    @pl.when(pl.program_id(0) == 0)
    def _():
        st_ref[...] = jnp.zeros_like(st_ref)

    rows = [lbraw_ref[r:r + 1, :] for r in range(lbraw_ref.shape[0])]
    mx = functools.reduce(jnp.maximum, rows)
    es = [jnp.exp(r - mx) for r in rows]
    lb = sum(es[1:layer + 1], jnp.zeros_like(mx)) / sum(es)

    f = f_ref[...]
    t = jnp.exp(-jnp.abs(f))
    r = 1.0 / (1.0 + t)
    pos = f >= 0.0
    tiny = jnp.logical_and(lb <= 0.0, jnp.logical_not(pos))
    num = jnp.where(pos, 1.0 + lb * t, lb + t)
    logf = jnp.log(jnp.where(tiny, 1.0, num) * r) + jnp.where(tiny, f, 0.0)
    key = (1.0 - lb) * jnp.where(pos, t * r, r)

    expo = _dot(mexp_ref[...], jnp.concatenate(_split_bf16(logf, 2), axis=0))

    lvl = lvl_ref[...]
    row = lax.broadcasted_iota(jnp.int32, (c, 1), 0)
    hh = range(heads)
    sl = {h: slice(h * dk, (h + 1) * dk) for h in hh}
    qh, kh, qb, kb, vh, bh, st, o, att = {}, {}, {}, {}, {}, {}, {}, {}, {}
    for h in hh:
        qv = q_ref[:, sl[h]]
        qh[h] = qv * _sigmoid(qv)
        kh[h] = key[:, sl[h]]
        qb[h], kb[h] = qh[h].astype(BF16), kh[h].astype(BF16)
        vh[h] = i_ref[:, sl[h]].astype(BF16)
        bh[h] = expo[0:c, sl[h]]
        st[h] = st_ref[h]
    for h in hh:
        o[h] = _dot_nt((qh[h] * jnp.exp(bh[h])).astype(BF16), st[h].astype(BF16))
        att[h] = jnp.where(lvl == n_lev, _dot_nt(qb[h], kb[h]), 0.0)
    for lev in range(n_lev):
        upper = (row & (1 << lev)) != 0
        for h in hh:
            x = jnp.where(upper, qb[h], kb[h]) * jnp.exp(expo[(1 + lev) * c:(2 + lev) * c, sl[h]]).astype(BF16)
            att[h] = jnp.where(lvl == lev, _dot_nt(x, x), att[h])
    for h in hh:
        o[h] = o[h] + _dot(att[h].astype(BF16), vh[h])
        khat = (kh[h] * jnp.exp(expo[(n_lev + 1) * c:(n_lev + 2) * c, sl[h]])).astype(BF16)
        st_ref[h] = st[h] * jnp.exp(bh[h][c - 1:c, :]) + _dot_tn(vh[h], khat)
    for h in hh:
        ogv = og_ref[:, sl[h]]
        o_ref[:, sl[h]] = (_rms(o[h], gn_ref[...]) * (ogv * _sigmoid(ogv))).astype(o_ref.dtype)


def _hgrn2_call(q, f, i, og, lbraw, g_norm, layer):
    t, d = q.shape
    c = HG_CHUNK
    mexp, lvl, n_lev = _hgrn2_tables(c)
    dk = d // HG_HEADS
    blk = pl.BlockSpec((c, d), lambda n: (n, 0))
    return pl.pallas_call(
        functools.partial(_hgrn2_kernel, layer=layer, n_lev=n_lev, heads=HG_HEADS),
        grid=(t // c,),
        in_specs=[_resident(lbraw.shape), _resident(g_norm.shape), _resident(mexp.shape),
                  _resident(lvl.shape), blk, blk, blk, blk],
        out_specs=blk, out_shape=jax.ShapeDtypeStruct((t, d), BF16),
        scratch_shapes=[pltpu.VMEM((HG_HEADS, dk, dk), F32)],
        compiler_params=_params(("arbitrary",)), name="hgrn2",
    )(lbraw, g_norm, mexp, lvl, q, f, i, og)


def _mlstm_kernel(tril_ref, norm_ref, q_ref, k_ref, v_ref, og_ref, g_ref, gt_ref, o_ref,
                  c_ref, n_ref, m_ref, *, heads):
    c = q_ref.shape[0]
    dqk = q_ref.shape[1] // heads
    dv = v_ref.shape[1] // heads

    @pl.when(pl.program_id(0) == 0)
    def _():
        c_ref[...] = jnp.zeros_like(c_ref)
        n_ref[...] = jnp.zeros_like(n_ref)
        m_ref[...] = jnp.zeros_like(m_ref)

    tril = tril_ref[...]
    gates = g_ref[...]
    gates_t = gt_ref[...]
    cum_col = sum(_dot(tril, part) for part in _split_bf16(gates, 3))
    cum_row = sum(_dot_nt(part, tril) for part in _split_bf16(gates_t, 3))
    causal = lax.broadcasted_iota(jnp.int32, (c, c), 0) >= lax.broadcasted_iota(jnp.int32, (c, c), 1)

    hh = range(heads)
    b_col = {h: cum_col[:, heads + h:heads + h + 1] for h in hh}
    b_row = {h: cum_row[heads + h:heads + h + 1, :] for h in hh}
    i_col = {h: gates[:, h:h + 1] for h in hh}
    i_row = {h: gates_t[h:h + 1, :] for h in hh}
    m_prev = {h: m_ref[h:h + 1, 0:1] for h in hh}
    qh = {h: q_ref[:, h * dqk:(h + 1) * dqk] for h in hh}
    kh = {h: k_ref[:, h * dqk:(h + 1) * dqk] * (dqk ** -0.5) for h in hh}
    vh = {h: v_ref[:, h * dv:(h + 1) * dv].astype(BF16) for h in hh}
    cst = {h: c_ref[h] for h in hh}
    nst = {h: n_ref[h:h + 1, :] for h in hh}
    qb = {h: qh[h].astype(BF16) for h in hh}

    log_intra = {h: jnp.where(causal, b_col[h] - b_row[h] + i_row[h], -jnp.inf) for h in hh}
    log_inter = {h: b_col[h] + m_prev[h] for h in hh}
    m_t = {h: jnp.maximum(jnp.max(log_intra[h], axis=1, keepdims=True), log_inter[h]) for h in hh}
    w_intra = {h: jnp.exp(log_intra[h] - m_t[h]) for h in hh}
    w_inter = {h: jnp.exp(log_inter[h] - m_t[h]) for h in hh}
    scores = {h: _dot_nt(qb[h], kh[h].astype(BF16)) * w_intra[h] for h in hh}
    inter = {h: _dot(qb[h], cst[h].astype(BF16)) for h in hh}
    num = {h: w_inter[h] * inter[h] + _dot(scores[h].astype(BF16), vh[h]) for h in hh}
    den = {h: w_inter[h] * jnp.sum(qh[h] * nst[h], axis=1, keepdims=True)
           + jnp.sum(scores[h], axis=1, keepdims=True) for h in hh}
    hid = {h: num[h] / jnp.maximum(jnp.abs(den[h]), jnp.exp(-m_t[h])) for h in hh}

    b_last = {h: b_col[h][c - 1:c, :] for h in hh}
    log_state = {h: b_last[h] + m_prev[h] for h in hh}
    m_new = {h: jnp.maximum(log_state[h], jnp.max(b_last[h] - b_row[h] + i_row[h], axis=1, keepdims=True))
             for h in hh}
    w_src = {h: jnp.exp(b_last[h] - b_col[h] + i_col[h] - m_new[h]) for h in hh}
    decay = {h: jnp.exp(log_state[h] - m_new[h]) for h in hh}
    khat = {h: kh[h] * w_src[h] for h in hh}
    for h in hh:
        c_ref[h] = decay[h] * cst[h] + _dot_tn(khat[h].astype(BF16), vh[h])
        n_ref[h:h + 1, :] = decay[h] * nst[h] + jnp.sum(khat[h], axis=0, keepdims=True)
        m_ref[h:h + 1, :] = jnp.broadcast_to(m_new[h], (1, m_ref.shape[1]))
    for h in hh:
        ogv = og_ref[:, h * dv:(h + 1) * dv]
        o_ref[:, h * dv:(h + 1) * dv] = (
            _rms(hid[h], norm_ref[:, h * dv:(h + 1) * dv]) * _sigmoid(ogv)).astype(o_ref.dtype)


def _mlstm_call(q, k, v, og, gates, gates_t, norm):
    t = q.shape[0]
    d = v.shape[1]
    c = ML_CHUNK
    dqk = q.shape[1] // ML_HEADS
    dv = d // ML_HEADS
    tril = jnp.asarray(np.tril(np.ones((c, c), np.float32)), BF16)
    blk = lambda n: pl.BlockSpec((c, n), lambda s: (s, 0))
    return pl.pallas_call(
        functools.partial(_mlstm_kernel, heads=ML_HEADS),
        grid=(t // c,),
        in_specs=[_resident(tril.shape), _resident(norm.shape), blk(q.shape[1]), blk(k.shape[1]),
                  blk(d), blk(d), blk(LANES), pl.BlockSpec((2 * ML_HEADS, c), lambda s: (0, s))],
        out_specs=blk(d), out_shape=jax.ShapeDtypeStruct((t, d), BF16),
        scratch_shapes=[pltpu.VMEM((ML_HEADS, dqk, dv), F32), pltpu.VMEM((8, dqk), F32),
                        pltpu.VMEM((8, LANES), F32)],
        compiler_params=_params(("arbitrary",)), name="mlstm",
    )(tril, norm, q, k, v, og, gates, gates_t)


def _swa_bias(blk):
    group = SW_HEADS // SW_KV_HEADS
    kpos = np.arange(2 * blk)[:, None] - blk
    dist = np.arange(blk)[None, :] - kpos
    slopes = 2.0 ** (-8.0 * (np.arange(SW_HEADS) + 1) / SW_HEADS)
    ok = (dist >= 0) & (dist < SW_WINDOW)
    tables = []
    for first in (True, False):
        valid = ok & ((kpos >= 0) | (not first))
        bias = np.where(valid[None], -slopes[:, None, None] * dist[None], -np.inf)
        bias = bias.reshape(SW_KV_HEADS, group, 2 * blk, blk).transpose(0, 2, 1, 3)
        tables.append(bias.reshape(SW_KV_HEADS, 2 * blk, group * blk))
    return jnp.asarray(np.stack(tables), F32)


def _lane_mean_sq(x, n):
    ones = jnp.ones((x.shape[1], x.shape[1]), BF16)
    hi, lo = _split_bf16(x * x, 2)
    return (_dot(hi, ones) + _dot(lo, ones)) * (1.0 / n)


def _swa_kernel(qg_ref, kg_ref, sink_ref, bias_ref, q_ref, kp_ref, kc_ref, vp_ref, vc_ref, o_ref):
    blk = q_ref.shape[0]
    group = SW_HEADS // SW_KV_HEADS
    lo = lax.broadcasted_iota(jnp.int32, (1, LANES), 1) < SW_HD
    kk = jnp.concatenate([kp_ref[...], kc_ref[...]], axis=0)
    vv = jnp.concatenate([vp_ref[...], vc_ref[...]], axis=0)
    q_scale = qg_ref[...] * (SW_HD ** -0.5)
    kn, v_low, v_high, qst = {}, {}, {}, {}
    for kg in range(SW_KV_HEADS // 2):
        kx = kk[:, kg * LANES:(kg + 1) * LANES]
        vx = vv[:, kg * LANES:(kg + 1) * LANES]
        kr = pltpu.roll(kx, SW_HD, axis=1)
        vr = pltpu.roll(vx, SW_HD, axis=1)
        for half in range(2):
            kv = 2 * kg + half
            k_dup = jnp.where(lo, kx, kr) if half == 0 else jnp.where(lo, kr, kx)
            kn[kv] = (k_dup * lax.rsqrt(_lane_mean_sq(k_dup, LANES) + EPS) * kg_ref[...]).astype(BF16)
            v_low[kv] = jnp.where(lo, vx if half == 0 else vr, 0.0).astype(BF16)
            v_high[kv] = jnp.where(lo, 0.0, vr if half == 0 else vx).astype(BF16)
    for kv in range(SW_KV_HEADS):
        qs = []
        for j in range(kv * group // 2, (kv + 1) * group // 2):
            qx = q_ref[:, j * LANES:(j + 1) * LANES]
            for part in (jnp.where(lo, qx, 0.0), jnp.where(lo, 0.0, qx)):
                qs.append((part * lax.rsqrt(_lane_mean_sq(part, SW_HD) + EPS) * q_scale).astype(BF16))
        qst[kv] = jnp.concatenate(qs, axis=0)
    s = {kv: _dot_nt(kn[kv], qst[kv]) + bias_ref[kv] for kv in range(SW_KV_HEADS)}
    pn = {}
    for kv in range(SW_KV_HEADS):
        sink = sink_ref[:, kv * group * blk:(kv + 1) * group * blk]
        mx = jnp.maximum(jnp.max(s[kv], axis=0, keepdims=True), sink)
        p = jnp.exp(s[kv] - mx)
        inv = 1.0 / (jnp.sum(p, axis=0, keepdims=True) + jnp.exp(sink - mx))
        pn[kv] = (p * inv).astype(BF16)
    for kv in range(SW_KV_HEADS):
        for idx, j in enumerate(range(kv * group // 2, (kv + 1) * group // 2)):
            even = pn[kv][:, 2 * idx * blk:(2 * idx + 1) * blk]
            odd = pn[kv][:, (2 * idx + 1) * blk:(2 * idx + 2) * blk]
            o_ref[:, j * LANES:(j + 1) * LANES] = (
                _dot_tn(even, v_low[kv]) + _dot_tn(odd, v_high[kv])).astype(o_ref.dtype)


def _swa_call(q, k, v, q_gain, k_gain, sinks):
    t, nq = q.shape
    nk = k.shape[1]
    b = SW_BLOCK
    bias = _swa_bias(b)
    qg2 = jnp.tile(q_gain, (1, LANES // SW_HD))
    kg2 = jnp.tile(k_gain, (1, LANES // SW_HD))
    sink_row = jnp.repeat(sinks, b, axis=1)
    cur = lambda n: pl.BlockSpec((b, n), lambda s: (s, 0))
    prev = lambda n: pl.BlockSpec((b, n), lambda s: (jnp.maximum(s - 1, 0), 0))
    bias_spec = pl.BlockSpec((None,) + bias.shape[1:], lambda s: (jnp.minimum(s, 1), 0, 0, 0))
    return pl.pallas_call(
        _swa_kernel, grid=(t // b,),
        in_specs=[_resident(qg2.shape), _resident(kg2.shape), _resident(sink_row.shape), bias_spec,
                  cur(nq), prev(nk), cur(nk), prev(nk), cur(nk)],
        out_specs=cur(nq), out_shape=jax.ShapeDtypeStruct((t, nq), BF16),
        compiler_params=_params(("arbitrary",)), name="swa",
    )(qg2, kg2, sink_row, bias, q, k, k, v, v)


def kernel(x, p, norm_gains, w_ffn_gu, w_ffn_down, w_ple_gate, w_ple_proj, hg_lower_bounds, hg_w_in,
           hg_g_norm, hg_w_out, ml_w_qkvo, ml_w_if, ml_b_if, ml_norm, ml_w_out, sw_w_qkv, sw_q_norm,
           sw_k_norm, sw_sinks, sw_w_o):
    batch, seq, d = x.shape
    depth = p.shape[0]
    wgu, wd = w_ffn_gu.astype(BF16), w_ffn_down.astype(BF16)
    wpg, wpp = w_ple_gate.astype(BF16), w_ple_proj.astype(BF16)
    w_in = (hg_w_in.astype(BF16), ml_w_qkvo.astype(BF16), sw_w_qkv.astype(BF16))
    w_out = (hg_w_out.astype(BF16), ml_w_out.astype(BF16), sw_w_o.astype(BF16))
    outs = []
    for bi in range(batch):
        xs, xb = x, bi
        for layer in range(depth):
            kind, j = layer % N_MIXERS, layer // N_MIXERS
            pre = functools.partial(_pre_call, xs, xb, norm_gains, layer, wgu, wd, w_in[kind], j)
            if kind == 0:
                h, q, f, i, og = pre((d, d, d, d), (F32, F32, F32, F32))
                mix = _hgrn2_call(q, f, i, og, hg_lower_bounds, hg_g_norm[j][None, :], layer)
            elif kind == 1:
                dqk = (ml_w_qkvo.shape[2] - 2 * d) // 2
                h, q, k, v, og, gates, gates_t = pre((dqk, dqk, d, d), (F32, F32, F32, F32),
                                                     gate_w=ml_w_if[j], gate_b=ml_b_if[j])
                mix = _mlstm_call(q, k, v, og, gates, gates_t, ml_norm[j][None, :])
            else:
                nq = SW_HEADS * SW_HD
                nk = SW_KV_HEADS * SW_HD
                h, q, k, v = pre((nq, nk, nk), (F32, F32, F32))
                mix = _swa_call(q, k, v, sw_q_norm[j][None, :], sw_k_norm[j][None, :], sw_sinks[j][None, :])
            xs = _post_call(h, mix, p, bi, norm_gains, layer, w_out[kind], j, wgu, wd, wpg, wpp)
            xb = 0
        outs.append(xs)
    return outs[0] if batch == 1 else jnp.concatenate(outs, axis=0)
```

```python
import functools

import numpy as np
import jax
import jax.numpy as jnp
from jax import lax
from jax.experimental import pallas as pl
from jax.experimental.pallas import tpu as pltpu

F32 = jnp.float32
BF16 = jnp.bfloat16

EPS = 1e-6
N_MIXERS = 3
HG_HEADS = 8
ML_HEADS = 4
ML_GATE_CAP = 15.0
SW_HEADS = 16
SW_KV_HEADS = 4
SW_HD = 64
SW_WINDOW = 128

V7X_VMEM_LIMIT_BYTES = 56 * 1024 * 1024
LANES = 128
BF16_SUBLANES = 16

ROW_TILE = 512
FF_CHUNK = 256
HG_CHUNK = 128
ML_CHUNK = 256
SW_BLOCK = 128


def _dot(a, b):
    return jnp.dot(a, b, preferred_element_type=F32)


def _dot_nt(a, b):
    return lax.dot_general(a, b, (((1,), (1,)), ((), ())), preferred_element_type=F32)


def _dot_tn(a, b):
    return lax.dot_general(a, b, (((0,), (0,)), ((), ())), preferred_element_type=F32)


def _sigmoid(x):
    return 0.5 * jnp.tanh(0.5 * x) + 0.5


def _log_sigmoid(x):
    return jnp.minimum(x, 0.0) - jnp.log1p(jnp.exp(-jnp.abs(x)))


def _rms(x, gain):
    return x * lax.rsqrt(jnp.mean(x * x, axis=-1, keepdims=True) + EPS) * gain


def _split_bf16(x, terms):
    parts = []
    r = x
    for _ in range(terms):
        p = r.astype(BF16)
        parts.append(p)
        r = r - p.astype(F32)
    return parts


def _swiglu_residual(x, gain, wgu_ref, wd_ref):
    d_ff = wd_ref.shape[0]
    xn = _rms(x, gain).astype(BF16)
    y = jnp.zeros_like(x)
    for j in range(d_ff // FF_CHUNK):
        lo = j * FF_CHUNK
        g = _dot(xn, wgu_ref[:, lo:lo + FF_CHUNK])
        u = _dot(xn, wgu_ref[:, d_ff + lo:d_ff + lo + FF_CHUNK])
        a = (g * _sigmoid(g) * u).astype(BF16)
        y = y + _dot(a, wd_ref[lo:lo + FF_CHUNK, :])
    return x + 0.5 * y


def _resident(shape, lead=()):
    nd = len(shape)
    block = (None,) * len(lead) + tuple(shape[len(lead):])
    index = tuple(lead) + (0,) * (nd - len(lead))
    return pl.BlockSpec(block, lambda *_: index, pipeline_mode=pl.Buffered(1))


def _params(semantics):
    return pltpu.CompilerParams(dimension_semantics=semantics,
                                vmem_limit_bytes=V7X_VMEM_LIMIT_BYTES)


def _cast_specs(jobs, n_steps):
    in_specs, out_specs, out_shapes = [], [], []
    for w, lead in jobs:
        rows, cols = w.shape[len(lead):]
        k = next(k for k in (1, 2, 4, 8) if (rows * k) % (n_steps * BF16_SUBLANES) == 0)
        hb = rows * k // n_steps
        in_specs.append(pl.BlockSpec((None,) * len(lead) + (hb, cols),
                                     lambda i, lead=tuple(lead), k=k: lead + (i // k, 0)))
        out_specs.append(pl.BlockSpec((hb, cols), lambda i, k=k: (i // k, 0)))
        out_shapes.append(jax.ShapeDtypeStruct((rows, cols), BF16))
    return in_specs, out_specs, out_shapes


def _cast_slabs(in_refs, out_refs):
    for src, dst in zip(in_refs, out_refs):
        dst[...] = src[...].astype(dst.dtype)


def _pre_kernel(*refs, splits, n_gate, n_cast):
    n_in = 5 + (2 if n_gate else 0)
    ins, cast_in, outs = refs[:n_in], refs[n_in:n_in + n_cast], refs[n_in + n_cast:]
    x_ref, gains_ref, wgu_ref, wd_ref, win_ref = ins[:5]
    outs, cast_out = outs[:len(outs) - n_cast], outs[len(outs) - n_cast:]
    _cast_slabs(cast_in, cast_out)
    h_ref = outs[0]
    proj_refs = outs[1:1 + len(splits)]
    h = _swiglu_residual(x_ref[...], gains_ref[0:1, :], wgu_ref, wd_ref)
    h_ref[...] = h
    xn = _rms(h, gains_ref[1:2, :])
    xb = xn.astype(BF16)
    off = 0
    for r, n in zip(proj_refs, splits):
        r[...] = _dot(xb, win_ref[:, off:off + n]).astype(r.dtype)
        off += n
    if n_gate:
        wif_ref, bif_ref = ins[5:7]
        g_ref, gt_ref = outs[1 + len(splits):]
        xh, xl = _split_bf16(xn, 2)
        wh, wl = _split_bf16(wif_ref[...], 2)
        pre = _dot(xh, wh) + _dot(xl, wh) + _dot(xh, wl) + bif_ref[...]
        capped = ML_GATE_CAP * jnp.tanh(pre / ML_GATE_CAP)
        lane = lax.broadcasted_iota(jnp.int32, capped.shape, 1)
        gates = jnp.where(lane < n_gate // 2, capped, _log_sigmoid(capped))
        g_ref[...] = gates
        gt_ref[...] = gates.T[0:gt_ref.shape[0], :]


def _pre_call(x, bi, gains, layer, weights, splits, out_dtypes, gate_w=None, gate_b=None, cast_jobs=()):
    _, t, d = x.shape
    n_gate = 0 if gate_w is None else gate_w.shape[1]
    tm = ROW_TILE
    row = lambda n: pl.BlockSpec((tm, n), lambda i: (i, 0))
    in_specs = [pl.BlockSpec((None, tm, d), lambda i: (bi, i, 0)), _resident(gains.shape, (layer,))]
    in_specs += [_resident(w.shape) for w in weights]
    args = [x, gains, *weights]
    out_shape = [jax.ShapeDtypeStruct((t, d), F32)]
    out_specs = [row(d)]
    for n, dt in zip(splits, out_dtypes):
        out_shape.append(jax.ShapeDtypeStruct((t, n), dt))
        out_specs.append(row(n))
    if n_gate:
        wpad = jnp.zeros((d, LANES), F32).at[:, :n_gate].set(gate_w)
        bpad = jnp.zeros((1, LANES), F32).at[0, :n_gate].set(gate_b)
        in_specs += [_resident(wpad.shape), _resident(bpad.shape)]
        args += [wpad, bpad]
        out_shape += [jax.ShapeDtypeStruct((t, LANES), F32), jax.ShapeDtypeStruct((n_gate, t), F32)]
        out_specs += [row(LANES), pl.BlockSpec((n_gate, tm), lambda i: (0, i))]
    c_in, c_out, c_shape = _cast_specs(cast_jobs, t // tm)
    return pl.pallas_call(
        functools.partial(_pre_kernel, splits=tuple(splits), n_gate=n_gate, n_cast=len(cast_jobs)),
        grid=(t // tm,), in_specs=in_specs + c_in, out_specs=out_specs + c_out, out_shape=out_shape + c_shape,
        compiler_params=_params(("arbitrary",)), name="pre",
    )(*args, *[w for w, _ in cast_jobs])


def _post_kernel(*refs, n_cast):
    h_ref, o_ref, p_ref, gains_ref, wout_ref, wgu_ref, wd_ref, wpg_ref, wpp_ref = refs[:9]
    cast_in, x_ref, cast_out = refs[9:9 + n_cast], refs[9 + n_cast], refs[10 + n_cast:]
    _cast_slabs(cast_in, cast_out)
    h = h_ref[...] + _dot(o_ref[...], wout_ref[...])
    h = _swiglu_residual(h, gains_ref[2:3, :], wgu_ref, wd_ref)
    gate = _sigmoid(_dot(_rms(h, gains_ref[3:4, :]).astype(BF16), wpg_ref[...]))
    x_ref[...] = h + gate * _dot(p_ref[...].astype(BF16), wpp_ref[...])


def _post_call(h, o, p, bi, gains, layer, weights, cast_jobs=()):
    t, d = h.shape
    tm = ROW_TILE
    row = lambda n: pl.BlockSpec((tm, n), lambda i: (i, 0))
    c_in, c_out, c_shape = _cast_specs(cast_jobs, t // tm)
    return pl.pallas_call(
        functools.partial(_post_kernel, n_cast=len(cast_jobs)), grid=(t // tm,),
        in_specs=[row(d), row(o.shape[1]), pl.BlockSpec((None, None, tm, p.shape[3]), lambda i: (layer, bi, i, 0)),
                  _resident(gains.shape, (layer,))] + [_resident(w.shape) for w in weights] + c_in,
        out_specs=[pl.BlockSpec((None, tm, d), lambda i: (0, i, 0))] + c_out,
        out_shape=[jax.ShapeDtypeStruct((1, t, d), F32)] + c_shape,
        compiler_params=_params(("arbitrary",)), name="post",
    )(h, o, p, gains, *weights, *[w for w, _ in cast_jobs])


def _hgrn2_tables(c):
    n_lev = int(np.log2(c))
    assert 2 ** n_lev == c
    mats = np.zeros((n_lev + 2, c, c), np.float32)
    r = np.arange(c)[:, None]
    j = np.arange(c)[None, :]
    mats[0] = j <= r
    for lev in range(n_lev):
        half = 2 ** lev
        m = (r // (2 * half)) * (2 * half) + half
        mats[1 + lev] = np.where(r >= m, (j >= m) & (j <= r), (j > r) & (j <= m - 1))
    mats[n_lev + 1] = j > r
    x = r ^ j
    lvl = np.where(r > j, np.floor(np.log2(np.maximum(x, 1))).astype(np.int32), -1)
    lvl = np.where(r == j, n_lev, lvl).astype(np.int32)
    mats = mats.reshape(-1, c)
    return jnp.asarray(np.concatenate([mats, mats], axis=1), BF16), jnp.asarray(lvl), n_lev


def _hgrn2_kernel(lbraw_ref, gn_ref, mexp_ref, lvl_ref, q_ref, f_ref, i_ref, og_ref, o_ref, st_ref,
                  *, layer, n_lev, heads):
    c, d = q_ref.shape
    dk = d // heads

    @pl.when(pl.program_id(0) == 0)
    def _():
        st_ref[...] = jnp.zeros_like(st_ref)

    rows = [lbraw_ref[r:r + 1, :] for r in range(lbraw_ref.shape[0])]
    mx = functools.reduce(jnp.maximum, rows)
    es = [jnp.exp(r - mx) for r in rows]
    lb = sum(es[1:layer + 1], jnp.zeros_like(mx)) / sum(es)

    f = f_ref[...]
    t = jnp.exp(-jnp.abs(f))
    r = 1.0 / (1.0 + t)
    pos = f >= 0.0
    tiny = jnp.logical_and(lb <= 0.0, jnp.logical_not(pos))
    num = jnp.where(pos, 1.0 + lb * t, lb + t)
    logf = jnp.log(jnp.where(tiny, 1.0, num) * r) + jnp.where(tiny, f, 0.0)
    key = (1.0 - lb) * jnp.where(pos, t * r, r)

    expo = _dot(mexp_ref[...], jnp.concatenate(_split_bf16(logf, 2), axis=0))

    lvl = lvl_ref[...]
    row = lax.broadcasted_iota(jnp.int32, (c, 1), 0)
    hh = range(heads)
    sl = {h: slice(h * dk, (h + 1) * dk) for h in hh}
    qh, kh, qb, kb, vh, bh, st, o, att = {}, {}, {}, {}, {}, {}, {}, {}, {}
    for h in hh:
        qv = q_ref[:, sl[h]]
        qh[h] = qv * _sigmoid(qv)
        kh[h] = key[:, sl[h]]
        qb[h], kb[h] = qh[h].astype(BF16), kh[h].astype(BF16)
        vh[h] = i_ref[:, sl[h]].astype(BF16)
        bh[h] = expo[0:c, sl[h]]
        st[h] = st_ref[h]
    for h in hh:
        o[h] = _dot_nt((qh[h] * jnp.exp(bh[h])).astype(BF16), st[h].astype(BF16))
        att[h] = jnp.where(lvl == n_lev, _dot_nt(qb[h], kb[h]), 0.0)
    for lev in range(n_lev):
        upper = (row & (1 << lev)) != 0
        for h in hh:
            x = jnp.where(upper, qb[h], kb[h]) * jnp.exp(expo[(1 + lev) * c:(2 + lev) * c, sl[h]]).astype(BF16)
            att[h] = jnp.where(lvl == lev, _dot_nt(x, x), att[h])
    for h in hh:
        o[h] = o[h] + _dot(att[h].astype(BF16), vh[h])
        khat = (kh[h] * jnp.exp(expo[(n_lev + 1) * c:(n_lev + 2) * c, sl[h]])).astype(BF16)
        st_ref[h] = st[h] * jnp.exp(bh[h][c - 1:c, :]) + _dot_tn(vh[h], khat)
    for h in hh:
        ogv = og_ref[:, sl[h]]
        o_ref[:, sl[h]] = (_rms(o[h], gn_ref[...]) * (ogv * _sigmoid(ogv))).astype(o_ref.dtype)


def _hgrn2_call(q, f, i, og, lbraw, g_norm, layer):
    t, d = q.shape
    c = HG_CHUNK
    mexp, lvl, n_lev = _hgrn2_tables(c)
    dk = d // HG_HEADS
    blk = pl.BlockSpec((c, d), lambda n: (n, 0))
    return pl.pallas_call(
        functools.partial(_hgrn2_kernel, layer=layer, n_lev=n_lev, heads=HG_HEADS),
        grid=(t // c,),
        in_specs=[_resident(lbraw.shape), _resident(g_norm.shape), _resident(mexp.shape),
                  _resident(lvl.shape), blk, blk, blk, blk],
        out_specs=blk, out_shape=jax.ShapeDtypeStruct((t, d), BF16),
        scratch_shapes=[pltpu.VMEM((HG_HEADS, dk, dk), F32)],
        compiler_params=_params(("arbitrary",)), name="hgrn2",
    )(lbraw, g_norm, mexp, lvl, q, f, i, og)


def _mlstm_kernel(tril_ref, norm_ref, q_ref, k_ref, v_ref, og_ref, g_ref, gt_ref, o_ref,
                  c_ref, n_ref, m_ref, *, heads):
    c = q_ref.shape[0]
    dqk = q_ref.shape[1] // heads
    dv = v_ref.shape[1] // heads

    @pl.when(pl.program_id(0) == 0)
    def _():
        c_ref[...] = jnp.zeros_like(c_ref)
        n_ref[...] = jnp.zeros_like(n_ref)
        m_ref[...] = jnp.zeros_like(m_ref)

    tril = tril_ref[...]
    gates = g_ref[...]
    gates_t = gt_ref[...]
    cum_col = sum(_dot(tril, part) for part in _split_bf16(gates, 3))
    cum_row = sum(_dot_nt(part, tril) for part in _split_bf16(gates_t, 3))
    causal = lax.broadcasted_iota(jnp.int32, (c, c), 0) >= lax.broadcasted_iota(jnp.int32, (c, c), 1)

    hh = range(heads)
    b_col = {h: cum_col[:, heads + h:heads + h + 1] for h in hh}
    b_row = {h: cum_row[heads + h:heads + h + 1, :] for h in hh}
    i_col = {h: gates[:, h:h + 1] for h in hh}
    i_row = {h: gates_t[h:h + 1, :] for h in hh}
    m_prev = {h: m_ref[h:h + 1, 0:1] for h in hh}
    qh = {h: q_ref[:, h * dqk:(h + 1) * dqk] for h in hh}
    kh = {h: k_ref[:, h * dqk:(h + 1) * dqk] * (dqk ** -0.5) for h in hh}
    vh = {h: v_ref[:, h * dv:(h + 1) * dv].astype(BF16) for h in hh}
    cst = {h: c_ref[h] for h in hh}
    nst = {h: n_ref[h:h + 1, :] for h in hh}
    qb = {h: qh[h].astype(BF16) for h in hh}

    log_intra = {h: jnp.where(causal, b_col[h] - b_row[h] + i_row[h], -jnp.inf) for h in hh}
    log_inter = {h: b_col[h] + m_prev[h] for h in hh}
    m_t = {h: jnp.maximum(jnp.max(log_intra[h], axis=1, keepdims=True), log_inter[h]) for h in hh}
    w_intra = {h: jnp.exp(log_intra[h] - m_t[h]) for h in hh}
    w_inter = {h: jnp.exp(log_inter[h] - m_t[h]) for h in hh}
    scores = {h: _dot_nt(qb[h], kh[h].astype(BF16)) * w_intra[h] for h in hh}
    inter = {h: _dot(qb[h], cst[h].astype(BF16)) for h in hh}
    num = {h: w_inter[h] * inter[h] + _dot(scores[h].astype(BF16), vh[h]) for h in hh}
    den = {h: w_inter[h] * jnp.sum(qh[h] * nst[h], axis=1, keepdims=True)
           + jnp.sum(scores[h], axis=1, keepdims=True) for h in hh}
    hid = {h: num[h] / jnp.maximum(jnp.abs(den[h]), jnp.exp(-m_t[h])) for h in hh}

    b_last = {h: b_col[h][c - 1:c, :] for h in hh}
    log_state = {h: b_last[h] + m_prev[h] for h in hh}
    m_new = {h: jnp.maximum(log_state[h], jnp.max(b_last[h] - b_row[h] + i_row[h], axis=1, keepdims=True))
             for h in hh}
    w_src = {h: jnp.exp(b_last[h] - b_col[h] + i_col[h] - m_new[h]) for h in hh}
    decay = {h: jnp.exp(log_state[h] - m_new[h]) for h in hh}
    khat = {h: kh[h] * w_src[h] for h in hh}
    for h in hh:
        c_ref[h] = decay[h] * cst[h] + _dot_tn(khat[h].astype(BF16), vh[h])
        n_ref[h:h + 1, :] = decay[h] * nst[h] + jnp.sum(khat[h], axis=0, keepdims=True)
        m_ref[h:h + 1, :] = jnp.broadcast_to(m_new[h], (1, m_ref.shape[1]))
    for h in hh:
        ogv = og_ref[:, h * dv:(h + 1) * dv]
        o_ref[:, h * dv:(h + 1) * dv] = (
            _rms(hid[h], norm_ref[:, h * dv:(h + 1) * dv]) * _sigmoid(ogv)).astype(o_ref.dtype)


def _mlstm_call(q, k, v, og, gates, gates_t, norm):
    t = q.shape[0]
    d = v.shape[1]
    c = ML_CHUNK
    dqk = q.shape[1] // ML_HEADS
    dv = d // ML_HEADS
    tril = jnp.asarray(np.tril(np.ones((c, c), np.float32)), BF16)
    blk = lambda n: pl.BlockSpec((c, n), lambda s: (s, 0))
    return pl.pallas_call(
        functools.partial(_mlstm_kernel, heads=ML_HEADS),
        grid=(t // c,),
        in_specs=[_resident(tril.shape), _resident(norm.shape), blk(q.shape[1]), blk(k.shape[1]),
                  blk(d), blk(d), blk(LANES), pl.BlockSpec((2 * ML_HEADS, c), lambda s: (0, s))],
        out_specs=blk(d), out_shape=jax.ShapeDtypeStruct((t, d), BF16),
        scratch_shapes=[pltpu.VMEM((ML_HEADS, dqk, dv), F32), pltpu.VMEM((8, dqk), F32),
                        pltpu.VMEM((8, LANES), F32)],
        compiler_params=_params(("arbitrary",)), name="mlstm",
    )(tril, norm, q, k, v, og, gates, gates_t)


def _swa_bias(blk):
    group = SW_HEADS // SW_KV_HEADS
    kpos = np.arange(2 * blk)[:, None] - blk
    dist = np.arange(blk)[None, :] - kpos
    slopes = 2.0 ** (-8.0 * (np.arange(SW_HEADS) + 1) / SW_HEADS)
    ok = (dist >= 0) & (dist < SW_WINDOW)
    tables = []
    for first in (True, False):
        valid = ok & ((kpos >= 0) | (not first))
        bias = np.where(valid[None], -slopes[:, None, None] * dist[None], -np.inf)
        bias = bias.reshape(SW_KV_HEADS, group, 2 * blk, blk).transpose(0, 2, 1, 3)
        tables.append(bias.reshape(SW_KV_HEADS, 2 * blk, group * blk))
    return jnp.asarray(np.stack(tables), F32)


def _lane_mean_sq(x, n):
    ones = jnp.ones((x.shape[1], x.shape[1]), BF16)
    hi, lo = _split_bf16(x * x, 2)
    return (_dot(hi, ones) + _dot(lo, ones)) * (1.0 / n)


def _swa_kernel(qg_ref, kg_ref, sink_ref, bias_ref, q_ref, kp_ref, kc_ref, vp_ref, vc_ref, o_ref):
    blk = q_ref.shape[0]
    group = SW_HEADS // SW_KV_HEADS
    lo = lax.broadcasted_iota(jnp.int32, (1, LANES), 1) < SW_HD
    kk = jnp.concatenate([kp_ref[...], kc_ref[...]], axis=0)
    vv = jnp.concatenate([vp_ref[...], vc_ref[...]], axis=0)
    q_scale = qg_ref[...] * (SW_HD ** -0.5)
    kn, v_low, v_high, qst = {}, {}, {}, {}
    for kg in range(SW_KV_HEADS // 2):
        kx = kk[:, kg * LANES:(kg + 1) * LANES]
        vx = vv[:, kg * LANES:(kg + 1) * LANES]
        kr = pltpu.roll(kx, SW_HD, axis=1)
        vr = pltpu.roll(vx, SW_HD, axis=1)
        for half in range(2):
            kv = 2 * kg + half
            k_dup = jnp.where(lo, kx, kr) if half == 0 else jnp.where(lo, kr, kx)
            kn[kv] = (k_dup * lax.rsqrt(_lane_mean_sq(k_dup, LANES) + EPS) * kg_ref[...]).astype(BF16)
            v_low[kv] = jnp.where(lo, vx if half == 0 else vr, 0.0).astype(BF16)
            v_high[kv] = jnp.where(lo, 0.0, vr if half == 0 else vx).astype(BF16)
    for kv in range(SW_KV_HEADS):
        qs = []
        for j in range(kv * group // 2, (kv + 1) * group // 2):
            qx = q_ref[:, j * LANES:(j + 1) * LANES]
            for part in (jnp.where(lo, qx, 0.0), jnp.where(lo, 0.0, qx)):
                qs.append((part * lax.rsqrt(_lane_mean_sq(part, SW_HD) + EPS) * q_scale).astype(BF16))
        qst[kv] = jnp.concatenate(qs, axis=0)
    s = {kv: _dot_nt(kn[kv], qst[kv]) + bias_ref[kv] for kv in range(SW_KV_HEADS)}
    pn = {}
    for kv in range(SW_KV_HEADS):
        sink = sink_ref[:, kv * group * blk:(kv + 1) * group * blk]
        mx = jnp.maximum(jnp.max(s[kv], axis=0, keepdims=True), sink)
        p = jnp.exp(s[kv] - mx)
        inv = 1.0 / (jnp.sum(p, axis=0, keepdims=True) + jnp.exp(sink - mx))
        pn[kv] = (p * inv).astype(BF16)
    for kv in range(SW_KV_HEADS):
        for idx, j in enumerate(range(kv * group // 2, (kv + 1) * group // 2)):
            even = pn[kv][:, 2 * idx * blk:(2 * idx + 1) * blk]
            odd = pn[kv][:, (2 * idx + 1) * blk:(2 * idx + 2) * blk]
            o_ref[:, j * LANES:(j + 1) * LANES] = (
                _dot_tn(even, v_low[kv]) + _dot_tn(odd, v_high[kv])).astype(o_ref.dtype)


def _swa_call(q, k, v, q_gain, k_gain, sinks):
    t, nq = q.shape
    nk = k.shape[1]
    b = SW_BLOCK
    bias = _swa_bias(b)
    qg2 = jnp.tile(q_gain, (1, LANES // SW_HD))
    kg2 = jnp.tile(k_gain, (1, LANES // SW_HD))
    sink_row = jnp.repeat(sinks, b, axis=1)
    cur = lambda n: pl.BlockSpec((b, n), lambda s: (s, 0))
    prev = lambda n: pl.BlockSpec((b, n), lambda s: (jnp.maximum(s - 1, 0), 0))
    bias_spec = pl.BlockSpec((None,) + bias.shape[1:], lambda s: (jnp.minimum(s, 1), 0, 0, 0))
    return pl.pallas_call(
        _swa_kernel, grid=(t // b,),
        in_specs=[_resident(qg2.shape), _resident(kg2.shape), _resident(sink_row.shape), bias_spec,
                  cur(nq), prev(nk), cur(nk), prev(nk), cur(nk)],
        out_specs=cur(nq), out_shape=jax.ShapeDtypeStruct((t, nq), BF16),
        compiler_params=_params(("arbitrary",)), name="swa",
    )(qg2, kg2, sink_row, bias, q, k, k, v, v)


def kernel(x, p, norm_gains, w_ffn_gu, w_ffn_down, w_ple_gate, w_ple_proj, hg_lower_bounds, hg_w_in,
           hg_g_norm, hg_w_out, ml_w_qkvo, ml_w_if, ml_b_if, ml_norm, ml_w_out, sw_w_qkv, sw_q_norm,
           sw_k_norm, sw_sinks, sw_w_o):
    batch, seq, d = x.shape
    depth = p.shape[0]
    w_in = (hg_w_in, ml_w_qkvo, sw_w_qkv)
    w_out = (hg_w_out, ml_w_out, sw_w_o)

    def pre_jobs(layer):
        return [(w_ffn_gu, (layer, 0)), (w_ffn_down, (layer, 0)), (w_in[layer % N_MIXERS], (layer // N_MIXERS,))]

    def post_jobs(layer):
        return [(w_out[layer % N_MIXERS], (layer // N_MIXERS,)), (w_ffn_gu, (layer, 1)), (w_ffn_down, (layer, 1)),
                (w_ple_gate, (layer,)), (w_ple_proj, (layer,))]

    outs = []
    for bi in range(batch):
        xs, xb = x, bi
        pre_w = [w[lead].astype(BF16) for w, lead in pre_jobs(0)]
        for layer in range(depth):
            kind, j = layer % N_MIXERS, layer // N_MIXERS
            pre = functools.partial(_pre_call, xs, xb, norm_gains, layer, pre_w, cast_jobs=post_jobs(layer))
            if kind == 0:
                h, q, f, i, og, *post_w = pre((d, d, d, d), (F32, F32, F32, F32))
                mix = _hgrn2_call(q, f, i, og, hg_lower_bounds, hg_g_norm[j][None, :], layer)
            elif kind == 1:
                dqk = (ml_w_qkvo.shape[2] - 2 * d) // 2
                h, q, k, v, og, gates, gates_t, *post_w = pre((dqk, dqk, d, d), (F32, F32, F32, F32),
                                                              gate_w=ml_w_if[j], gate_b=ml_b_if[j])
                mix = _mlstm_call(q, k, v, og, gates, gates_t, ml_norm[j][None, :])
            else:
                nq = SW_HEADS * SW_HD
                nk = SW_KV_HEADS * SW_HD
                h, q, k, v, *post_w = pre((nq, nk, nk), (F32, F32, F32))
                mix = _swa_call(q, k, v, sw_q_norm[j][None, :], sw_k_norm[j][None, :], sw_sinks[j][None, :])
            jobs = pre_jobs(layer + 1) if layer + 1 < depth else []
            xs, *pre_w = _post_call(h, mix, p, bi, norm_gains, layer, post_w, cast_jobs=jobs)
            xb = 0
        outs.append(xs)
    return outs[0] if batch == 1 else jnp.concatenate(outs, axis=0)
```

```python
import functools

import numpy as np
import jax
import jax.numpy as jnp
from jax import lax
from jax.experimental import pallas as pl
from jax.experimental.pallas import tpu as pltpu

F32 = jnp.float32
BF16 = jnp.bfloat16

EPS = 1e-6
N_MIXERS = 3
HG_HEADS = 8
ML_HEADS = 4
ML_GATE_CAP = 15.0
SW_HEADS = 16
SW_KV_HEADS = 4
SW_HD = 64
SW_WINDOW = 128

V7X_VMEM_LIMIT_BYTES = 56 * 1024 * 1024
LANES = 128
BF16_SUBLANES = 16

ROW_TILE = 512
FF_CHUNK = 256
HG_CHUNK = 128
HG_TILE = 256
ML_CHUNK = 256
SW_BLOCK = 128
SW_TILE = 256


def _dot(a, b):
    return jnp.dot(a, b, preferred_element_type=F32)


def _dot_nt(a, b):
    return lax.dot_general(a, b, (((1,), (1,)), ((), ())), preferred_element_type=F32)


def _dot_tn(a, b):
    return lax.dot_general(a, b, (((0,), (0,)), ((), ())), preferred_element_type=F32)


def _sigmoid(x):
    return 0.5 * jnp.tanh(0.5 * x) + 0.5


def _log_sigmoid(x):
    return jnp.minimum(x, 0.0) - jnp.log1p(jnp.exp(-jnp.abs(x)))


def _rms(x, gain):
    return x * lax.rsqrt(jnp.mean(x * x, axis=-1, keepdims=True) + EPS) * gain


def _split_bf16(x, terms):
    parts = []
    r = x
    for _ in range(terms):
        p = r.astype(BF16)
        parts.append(p)
        r = r - p.astype(F32)
    return parts


def _swiglu_residual(x, gain, wgu_ref, wd_ref):
    d_ff = wd_ref.shape[0]
    xn = _rms(x, gain).astype(BF16)
    y = jnp.zeros_like(x)
    for j in range(d_ff // FF_CHUNK):
        lo = j * FF_CHUNK
        g = _dot(xn, wgu_ref[:, lo:lo + FF_CHUNK])
        u = _dot(xn, wgu_ref[:, d_ff + lo:d_ff + lo + FF_CHUNK])
        a = (g * _sigmoid(g) * u).astype(BF16)
        y = y + _dot(a, wd_ref[lo:lo + FF_CHUNK, :])
    return x + 0.5 * y


def _resident(shape, lead=()):
    nd = len(shape)
    block = (None,) * len(lead) + tuple(shape[len(lead):])
    index = tuple(lead) + (0,) * (nd - len(lead))
    return pl.BlockSpec(block, lambda *_: index, pipeline_mode=pl.Buffered(1))


def _params(semantics):
    return pltpu.CompilerParams(dimension_semantics=semantics,
                                vmem_limit_bytes=V7X_VMEM_LIMIT_BYTES)


def _cast_specs(jobs, n_steps):
    in_specs, out_specs, out_shapes = [], [], []
    for w, lead in jobs:
        rows, cols = w.shape[len(lead):]
        k = next(k for k in (1, 2, 4, 8) if (rows * k) % (n_steps * BF16_SUBLANES) == 0)
        hb = rows * k // n_steps
        in_specs.append(pl.BlockSpec((None,) * len(lead) + (hb, cols),
                                     lambda i, lead=tuple(lead), k=k: lead + (i // k, 0)))
        out_specs.append(pl.BlockSpec((hb, cols), lambda i, k=k: (i // k, 0)))
        out_shapes.append(jax.ShapeDtypeStruct((rows, cols), BF16))
    return in_specs, out_specs, out_shapes


def _cast_slabs(in_refs, out_refs):
    for src, dst in zip(in_refs, out_refs):
        dst[...] = src[...].astype(dst.dtype)


def _pre_kernel(*refs, splits, n_gate, n_cast):
    n_in = 5 + (2 if n_gate else 0)
    ins, cast_in, outs = refs[:n_in], refs[n_in:n_in + n_cast], refs[n_in + n_cast:]
    x_ref, gains_ref, wgu_ref, wd_ref, win_ref = ins[:5]
    outs, cast_out = outs[:len(outs) - n_cast], outs[len(outs) - n_cast:]
    _cast_slabs(cast_in, cast_out)
    h_ref = outs[0]
    proj_refs = outs[1:1 + len(splits)]
    h = _swiglu_residual(x_ref[...], gains_ref[0:1, :], wgu_ref, wd_ref)
    h_ref[...] = h
    xn = _rms(h, gains_ref[1:2, :])
    xb = xn.astype(BF16)
    off = 0
    for r, n in zip(proj_refs, splits):
        r[...] = _dot(xb, win_ref[:, off:off + n]).astype(r.dtype)
        off += n
    if n_gate:
        wif_ref, bif_ref = ins[5:7]
        g_ref, gt_ref = outs[1 + len(splits):]
        xh, xl = _split_bf16(xn, 2)
        wh, wl = _split_bf16(wif_ref[...], 2)
        pre = _dot(xh, wh) + _dot(xl, wh) + _dot(xh, wl) + bif_ref[...]
        capped = ML_GATE_CAP * jnp.tanh(pre / ML_GATE_CAP)
        lane = lax.broadcasted_iota(jnp.int32, capped.shape, 1)
        gates = jnp.where(lane < n_gate // 2, capped, _log_sigmoid(capped))
        g_ref[...] = gates
        gt_ref[...] = gates.T[0:gt_ref.shape[0], :]


def _pre_call(x, bi, gains, layer, weights, splits, out_dtypes, gate_w=None, gate_b=None, cast_jobs=()):
    _, t, d = x.shape
    n_gate = 0 if gate_w is None else gate_w.shape[1]
    tm = ROW_TILE
    row = lambda n: pl.BlockSpec((tm, n), lambda i: (i, 0))
    in_specs = [pl.BlockSpec((None, tm, d), lambda i: (bi, i, 0)), _resident(gains.shape, (layer,))]
    in_specs += [_resident(w.shape) for w in weights]
    args = [x, gains, *weights]
    out_shape = [jax.ShapeDtypeStruct((t, d), F32)]
    out_specs = [row(d)]
    for n, dt in zip(splits, out_dtypes):
        out_shape.append(jax.ShapeDtypeStruct((t, n), dt))
        out_specs.append(row(n))
    if n_gate:
        wpad = jnp.zeros((d, LANES), F32).at[:, :n_gate].set(gate_w)
        bpad = jnp.zeros((1, LANES), F32).at[0, :n_gate].set(gate_b)
        in_specs += [_resident(wpad.shape), _resident(bpad.shape)]
        args += [wpad, bpad]
        out_shape += [jax.ShapeDtypeStruct((t, LANES), F32), jax.ShapeDtypeStruct((n_gate, t), F32)]
        out_specs += [row(LANES), pl.BlockSpec((n_gate, tm), lambda i: (0, i))]
    c_in, c_out, c_shape = _cast_specs(cast_jobs, t // tm)
    return pl.pallas_call(
        functools.partial(_pre_kernel, splits=tuple(splits), n_gate=n_gate, n_cast=len(cast_jobs)),
        grid=(t // tm,), in_specs=in_specs + c_in, out_specs=out_specs + c_out, out_shape=out_shape + c_shape,
        compiler_params=_params(("arbitrary",)), name="pre",
    )(*args, *[w for w, _ in cast_jobs])


def _post_kernel(*refs, n_cast):
    h_ref, o_ref, p_ref, gains_ref, wout_ref, wgu_ref, wd_ref, wpg_ref, wpp_ref = refs[:9]
    cast_in, x_ref, cast_out = refs[9:9 + n_cast], refs[9 + n_cast], refs[10 + n_cast:]
    _cast_slabs(cast_in, cast_out)
    h = h_ref[...] + _dot(o_ref[...], wout_ref[...])
    h = _swiglu_residual(h, gains_ref[2:3, :], wgu_ref, wd_ref)
    gate = _sigmoid(_dot(_rms(h, gains_ref[3:4, :]).astype(BF16), wpg_ref[...]))
    x_ref[...] = h + gate * _dot(p_ref[...].astype(BF16), wpp_ref[...])


def _post_call(h, o, p, bi, gains, layer, weights, cast_jobs=()):
    t, d = h.shape
    tm = ROW_TILE
    row = lambda n: pl.BlockSpec((tm, n), lambda i: (i, 0))
    c_in, c_out, c_shape = _cast_specs(cast_jobs, t // tm)
    return pl.pallas_call(
        functools.partial(_post_kernel, n_cast=len(cast_jobs)), grid=(t // tm,),
        in_specs=[row(d), row(o.shape[1]), pl.BlockSpec((None, None, tm, p.shape[3]), lambda i: (layer, bi, i, 0)),
                  _resident(gains.shape, (layer,))] + [_resident(w.shape) for w in weights] + c_in,
        out_specs=[pl.BlockSpec((None, tm, d), lambda i: (0, i, 0))] + c_out,
        out_shape=[jax.ShapeDtypeStruct((1, t, d), F32)] + c_shape,
        compiler_params=_params(("arbitrary",)), name="post",
    )(h, o, p, gains, *weights, *[w for w, _ in cast_jobs])


def _hgrn2_tables(c):
    n_lev = int(np.log2(c))
    assert 2 ** n_lev == c
    mats = np.zeros((n_lev + 2, c, c), np.float32)
    r = np.arange(c)[:, None]
    j = np.arange(c)[None, :]
    mats[0] = j <= r
    for lev in range(n_lev):
        half = 2 ** lev
        m = (r // (2 * half)) * (2 * half) + half
        mats[1 + lev] = np.where(r >= m, (j >= m) & (j <= r), (j > r) & (j <= m - 1))
    mats[n_lev + 1] = j > r
    x = r ^ j
    lvl = np.where(r > j, np.floor(np.log2(np.maximum(x, 1))).astype(np.int32), -1)
    lvl = np.where(r == j, n_lev, lvl).astype(np.int32)
    mats = mats.reshape(-1, c)
    return jnp.asarray(np.concatenate([mats, mats], axis=1), BF16), jnp.asarray(lvl), n_lev


def _hgrn2_kernel(lbraw_ref, gn_ref, mexp_ref, lvl_ref, q_ref, f_ref, i_ref, og_ref, o_ref, st_ref,
                  *, layer, n_lev, heads, c):
    rows, d = q_ref.shape
    dk = d // heads
    chunks = range(rows // c)
    hh = range(heads)
    rs = {ci: slice(ci * c, (ci + 1) * c) for ci in chunks}
    sl = {h: slice(h * dk, (h + 1) * dk) for h in hh}
    pairs = [(ci, h) for ci in chunks for h in hh]

    @pl.when(pl.program_id(0) == 0)
    def _():
        st_ref[...] = jnp.zeros_like(st_ref)

    lrows = [lbraw_ref[r:r + 1, :] for r in range(lbraw_ref.shape[0])]
    mx = functools.reduce(jnp.maximum, lrows)
    es = [jnp.exp(r - mx) for r in lrows]
    lb = sum(es[1:layer + 1], jnp.zeros_like(mx)) / sum(es)

    key, expo = {}, {}
    for ci in chunks:
        f = f_ref[rs[ci], :]
        t = jnp.exp(-jnp.abs(f))
        r = 1.0 / (1.0 + t)
        pos = f >= 0.0
        tiny = jnp.logical_and(lb <= 0.0, jnp.logical_not(pos))
        num = jnp.where(pos, 1.0 + lb * t, lb + t)
        logf = jnp.log(jnp.where(tiny, 1.0, num) * r) + jnp.where(tiny, f, 0.0)
        key[ci] = (1.0 - lb) * jnp.where(pos, t * r, r)
        expo[ci] = _dot(mexp_ref[...], jnp.concatenate(_split_bf16(logf, 2), axis=0))

    lvl = lvl_ref[...]
    row = lax.broadcasted_iota(jnp.int32, (c, 1), 0)
    qh, kh, qb, kb, vh, bh, att, o = {}, {}, {}, {}, {}, {}, {}, {}
    for ci, h in pairs:
        qv = q_ref[rs[ci], sl[h]]
        qh[ci, h] = qv * _sigmoid(qv)
        kh[ci, h] = key[ci][:, sl[h]]
        qb[ci, h], kb[ci, h] = qh[ci, h].astype(BF16), kh[ci, h].astype(BF16)
        vh[ci, h] = i_ref[rs[ci], sl[h]].astype(BF16)
        bh[ci, h] = expo[ci][0:c, sl[h]]
    for ci, h in pairs:
        att[ci, h] = jnp.where(lvl == n_lev, _dot_nt(qb[ci, h], kb[ci, h]), 0.0)
    for lev in range(n_lev):
        upper = (row & (1 << lev)) != 0
        for ci, h in pairs:
            e = jnp.exp(expo[ci][(1 + lev) * c:(2 + lev) * c, sl[h]]).astype(BF16)
            x = jnp.where(upper, qb[ci, h], kb[ci, h]) * e
            att[ci, h] = jnp.where(lvl == lev, _dot_nt(x, x), att[ci, h])
    st = {h: st_ref[h] for h in hh}
    for ci in chunks:
        for h in hh:
            o[ci, h] = (_dot_nt((qh[ci, h] * jnp.exp(bh[ci, h])).astype(BF16), st[h].astype(BF16))
                        + _dot(att[ci, h].astype(BF16), vh[ci, h]))
        for h in hh:
            khat = (kh[ci, h] * jnp.exp(expo[ci][(n_lev + 1) * c:(n_lev + 2) * c, sl[h]])).astype(BF16)
            st[h] = st[h] * jnp.exp(bh[ci, h][c - 1:c, :]) + _dot_tn(vh[ci, h], khat)
    for h in hh:
        st_ref[h] = st[h]
    for ci, h in pairs:
        ogv = og_ref[rs[ci], sl[h]]
        o_ref[rs[ci], sl[h]] = (_rms(o[ci, h], gn_ref[...]) * (ogv * _sigmoid(ogv))).astype(o_ref.dtype)


def _hgrn2_call(q, f, i, og, lbraw, g_norm, layer):
    t, d = q.shape
    c = HG_CHUNK
    tile = HG_TILE
    mexp, lvl, n_lev = _hgrn2_tables(c)
    dk = d // HG_HEADS
    blk = pl.BlockSpec((tile, d), lambda n: (n, 0))
    return pl.pallas_call(
        functools.partial(_hgrn2_kernel, layer=layer, n_lev=n_lev, heads=HG_HEADS, c=c),
        grid=(t // tile,),
        in_specs=[_resident(lbraw.shape), _resident(g_norm.shape), _resident(mexp.shape),
                  _resident(lvl.shape), blk, blk, blk, blk],
        out_specs=blk, out_shape=jax.ShapeDtypeStruct((t, d), BF16),
        scratch_shapes=[pltpu.VMEM((HG_HEADS, dk, dk), F32)],
        compiler_params=_params(("arbitrary",)), name="hgrn2",
    )(lbraw, g_norm, mexp, lvl, q, f, i, og)


def _mlstm_kernel(tril_ref, norm_ref, q_ref, k_ref, v_ref, og_ref, g_ref, gt_ref, o_ref,
                  c_ref, n_ref, m_ref, *, heads):
    c = q_ref.shape[0]
    dqk = q_ref.shape[1] // heads
    dv = v_ref.shape[1] // heads

    @pl.when(pl.program_id(0) == 0)
    def _():
        c_ref[...] = jnp.zeros_like(c_ref)
        n_ref[...] = jnp.zeros_like(n_ref)
        m_ref[...] = jnp.zeros_like(m_ref)

    tril = tril_ref[...]
    gates = g_ref[...]
    gates_t = gt_ref[...]
    cum_col = sum(_dot(tril, part) for part in _split_bf16(gates, 3))
    cum_row = sum(_dot_nt(part, tril) for part in _split_bf16(gates_t, 3))
    causal = lax.broadcasted_iota(jnp.int32, (c, c), 0) >= lax.broadcasted_iota(jnp.int32, (c, c), 1)

    hh = range(heads)
    b_col = {h: cum_col[:, heads + h:heads + h + 1] for h in hh}
    b_row = {h: cum_row[heads + h:heads + h + 1, :] for h in hh}
    i_col = {h: gates[:, h:h + 1] for h in hh}
    i_row = {h: gates_t[h:h + 1, :] for h in hh}
    m_prev = {h: m_ref[h:h + 1, 0:1] for h in hh}
    qh = {h: q_ref[:, h * dqk:(h + 1) * dqk] for h in hh}
    kh = {h: k_ref[:, h * dqk:(h + 1) * dqk] * (dqk ** -0.5) for h in hh}
    vh = {h: v_ref[:, h * dv:(h + 1) * dv].astype(BF16) for h in hh}
    cst = {h: c_ref[h] for h in hh}
    nst = {h: n_ref[h:h + 1, :] for h in hh}
    qb = {h: qh[h].astype(BF16) for h in hh}

    log_intra = {h: jnp.where(causal, b_col[h] - b_row[h] + i_row[h], -jnp.inf) for h in hh}
    log_inter = {h: b_col[h] + m_prev[h] for h in hh}
    m_t = {h: jnp.maximum(jnp.max(log_intra[h], axis=1, keepdims=True), log_inter[h]) for h in hh}
    w_intra = {h: jnp.exp(log_intra[h] - m_t[h]) for h in hh}
    w_inter = {h: jnp.exp(log_inter[h] - m_t[h]) for h in hh}
    scores = {h: _dot_nt(qb[h], kh[h].astype(BF16)) * w_intra[h] for h in hh}
    inter = {h: _dot(qb[h], cst[h].astype(BF16)) for h in hh}
    num = {h: w_inter[h] * inter[h] + _dot(scores[h].astype(BF16), vh[h]) for h in hh}
    den = {h: w_inter[h] * jnp.sum(qh[h] * nst[h], axis=1, keepdims=True)
           + jnp.sum(scores[h], axis=1, keepdims=True) for h in hh}
    hid = {h: num[h] / jnp.maximum(jnp.abs(den[h]), jnp.exp(-m_t[h])) for h in hh}

    b_last = {h: b_col[h][c - 1:c, :] for h in hh}
    log_state = {h: b_last[h] + m_prev[h] for h in hh}
    m_new = {h: jnp.maximum(log_state[h], jnp.max(b_last[h] - b_row[h] + i_row[h], axis=1, keepdims=True))
             for h in hh}
    w_src = {h: jnp.exp(b_last[h] - b_col[h] + i_col[h] - m_new[h]) for h in hh}
    decay = {h: jnp.exp(log_state[h] - m_new[h]) for h in hh}
    khat = {h: kh[h] * w_src[h] for h in hh}
    for h in hh:
        c_ref[h] = decay[h] * cst[h] + _dot_tn(khat[h].astype(BF16), vh[h])
        n_ref[h:h + 1, :] = decay[h] * nst[h] + jnp.sum(khat[h], axis=0, keepdims=True)
        m_ref[h:h + 1, :] = jnp.broadcast_to(m_new[h], (1, m_ref.shape[1]))
    for h in hh:
        ogv = og_ref[:, h * dv:(h + 1) * dv]
        o_ref[:, h * dv:(h + 1) * dv] = (
            _rms(hid[h], norm_ref[:, h * dv:(h + 1) * dv]) * _sigmoid(ogv)).astype(o_ref.dtype)


def _mlstm_call(q, k, v, og, gates, gates_t, norm):
    t = q.shape[0]
    d = v.shape[1]
    c = ML_CHUNK
    dqk = q.shape[1] // ML_HEADS
    dv = d // ML_HEADS
    tril = jnp.asarray(np.tril(np.ones((c, c), np.float32)), BF16)
    blk = lambda n: pl.BlockSpec((c, n), lambda s: (s, 0))
    return pl.pallas_call(
        functools.partial(_mlstm_kernel, heads=ML_HEADS),
        grid=(t // c,),
        in_specs=[_resident(tril.shape), _resident(norm.shape), blk(q.shape[1]), blk(k.shape[1]),
                  blk(d), blk(d), blk(LANES), pl.BlockSpec((2 * ML_HEADS, c), lambda s: (0, s))],
        out_specs=blk(d), out_shape=jax.ShapeDtypeStruct((t, d), BF16),
        scratch_shapes=[pltpu.VMEM((ML_HEADS, dqk, dv), F32), pltpu.VMEM((8, dqk), F32),
                        pltpu.VMEM((8, LANES), F32)],
        compiler_params=_params(("arbitrary",)), name="mlstm",
    )(tril, norm, q, k, v, og, gates, gates_t)


def _swa_bias(blk):
    group = SW_HEADS // SW_KV_HEADS
    kpos = np.arange(2 * blk)[:, None] - blk
    dist = np.arange(blk)[None, :] - kpos
    slopes = 2.0 ** (-8.0 * (np.arange(SW_HEADS) + 1) / SW_HEADS)
    ok = (dist >= 0) & (dist < SW_WINDOW)
    tables = []
    for first in (True, False):
        valid = ok & ((kpos >= 0) | (not first))
        bias = np.where(valid[None], -slopes[:, None, None] * dist[None], -np.inf)
        bias = bias.reshape(SW_KV_HEADS, group, 2 * blk, blk).transpose(0, 2, 1, 3)
        tables.append(bias.reshape(SW_KV_HEADS, 2 * blk, group * blk))
    return jnp.asarray(np.stack(tables), F32)


def _lane_mean_sq(x, n):
    ones = jnp.ones((x.shape[1], x.shape[1]), BF16)
    hi, lo = _split_bf16(x * x, 2)
    return (_dot(hi, ones) + _dot(lo, ones)) * (1.0 / n)


def _swa_kernel(qg_ref, kg_ref, sink_ref, bias_ref, q_ref, kp_ref, kc_ref, vp_ref, vc_ref, o_ref, *, blk):
    group = SW_HEADS // SW_KV_HEADS
    n_blk = q_ref.shape[0] // blk
    lo = lax.broadcasted_iota(jnp.int32, (1, LANES), 1) < SW_HD
    kk = jnp.concatenate([kp_ref[...], kc_ref[...]], axis=0)
    vv = jnp.concatenate([vp_ref[...], vc_ref[...]], axis=0)
    q_scale = qg_ref[...] * (SW_HD ** -0.5)
    kn, v_low, v_high, qst = {}, {}, {}, {}
    for kg in range(SW_KV_HEADS // 2):
        kx = kk[:, kg * LANES:(kg + 1) * LANES]
        vx = vv[:, kg * LANES:(kg + 1) * LANES]
        kr = pltpu.roll(kx, SW_HD, axis=1)
        vr = pltpu.roll(vx, SW_HD, axis=1)
        for half in range(2):
            kv = 2 * kg + half
            k_dup = jnp.where(lo, kx, kr) if half == 0 else jnp.where(lo, kr, kx)
            kn[kv] = (k_dup * lax.rsqrt(_lane_mean_sq(k_dup, LANES) + EPS) * kg_ref[...]).astype(BF16)
            v_low[kv] = jnp.where(lo, vx if half == 0 else vr, 0.0).astype(BF16)
            v_high[kv] = jnp.where(lo, 0.0, vr if half == 0 else vx).astype(BF16)
    pairs = [(b, kv) for b in range(n_blk) for kv in range(SW_KV_HEADS)]
    for b, kv in pairs:
        qs = []
        for j in range(kv * group // 2, (kv + 1) * group // 2):
            qx = q_ref[b * blk:(b + 1) * blk, j * LANES:(j + 1) * LANES]
            for part in (jnp.where(lo, qx, 0.0), jnp.where(lo, 0.0, qx)):
                qs.append((part * lax.rsqrt(_lane_mean_sq(part, SW_HD) + EPS) * q_scale).astype(BF16))
        qst[b, kv] = jnp.concatenate(qs, axis=0)
    s = {(b, kv): _dot_nt(kn[kv][b * blk:(b + 2) * blk], qst[b, kv]) + bias_ref[0 if b == 0 else 1, kv]
         for b, kv in pairs}
    pn = {}
    for b, kv in pairs:
        sink = sink_ref[:, kv * group * blk:(kv + 1) * group * blk]
        mx = jnp.maximum(jnp.max(s[b, kv], axis=0, keepdims=True), sink)
        p = jnp.exp(s[b, kv] - mx)
        inv = 1.0 / (jnp.sum(p, axis=0, keepdims=True) + jnp.exp(sink - mx))
        pn[b, kv] = (p * inv).astype(BF16)
    for b, kv in pairs:
        ks = slice(b * blk, (b + 2) * blk)
        for idx, j in enumerate(range(kv * group // 2, (kv + 1) * group // 2)):
            even = pn[b, kv][:, 2 * idx * blk:(2 * idx + 1) * blk]
            odd = pn[b, kv][:, (2 * idx + 1) * blk:(2 * idx + 2) * blk]
            o_ref[b * blk:(b + 1) * blk, j * LANES:(j + 1) * LANES] = (
                _dot_tn(even, v_low[kv][ks]) + _dot_tn(odd, v_high[kv][ks])).astype(o_ref.dtype)


def _swa_call(q, k, v, q_gain, k_gain, sinks):
    t, nq = q.shape
    nk = k.shape[1]
    b = SW_BLOCK
    n_blk = SW_TILE // b
    bias = _swa_bias(b)
    qg2 = jnp.tile(q_gain, (1, LANES // SW_HD))
    kg2 = jnp.tile(k_gain, (1, LANES // SW_HD))
    sink_row = jnp.repeat(sinks, b, axis=1)
    cur = lambda n: pl.BlockSpec((SW_TILE, n), lambda s: (s, 0))
    prev = lambda n: pl.BlockSpec((b, n), lambda s: (jnp.maximum(s * n_blk - 1, 0), 0))
    bias2 = jnp.stack([bias, jnp.stack([bias[1], bias[1]])])
    bias_spec = pl.BlockSpec((None,) + bias2.shape[1:], lambda s: (jnp.minimum(s, 1), 0, 0, 0, 0))
    return pl.pallas_call(
        functools.partial(_swa_kernel, blk=b), grid=(t // SW_TILE,),
        in_specs=[_resident(qg2.shape), _resident(kg2.shape), _resident(sink_row.shape), bias_spec,
                  cur(nq), prev(nk), cur(nk), prev(nk), cur(nk)],
        out_specs=cur(nq), out_shape=jax.ShapeDtypeStruct((t, nq), BF16),
        compiler_params=_params(("arbitrary",)), name="swa",
    )(qg2, kg2, sink_row, bias2, q, k, k, v, v)


def kernel(x, p, norm_gains, w_ffn_gu, w_ffn_down, w_ple_gate, w_ple_proj, hg_lower_bounds, hg_w_in,
           hg_g_norm, hg_w_out, ml_w_qkvo, ml_w_if, ml_b_if, ml_norm, ml_w_out, sw_w_qkv, sw_q_norm,
           sw_k_norm, sw_sinks, sw_w_o):
    batch, seq, d = x.shape
    depth = p.shape[0]
    w_in = (hg_w_in, ml_w_qkvo, sw_w_qkv)
    w_out = (hg_w_out, ml_w_out, sw_w_o)

    def pre_jobs(layer):
        return [(w_ffn_gu, (layer, 0)), (w_ffn_down, (layer, 0)), (w_in[layer % N_MIXERS], (layer // N_MIXERS,))]

    def post_jobs(layer):
        return [(w_out[layer % N_MIXERS], (layer // N_MIXERS,)), (w_ffn_gu, (layer, 1)), (w_ffn_down, (layer, 1)),
                (w_ple_gate, (layer,)), (w_ple_proj, (layer,))]

    outs = []
    for bi in range(batch):
        xs, xb = x, bi
        pre_w = [w[lead].astype(BF16) for w, lead in pre_jobs(0)]
        for layer in range(depth):
            kind, j = layer % N_MIXERS, layer // N_MIXERS
            pre = functools.partial(_pre_call, xs, xb, norm_gains, layer, pre_w, cast_jobs=post_jobs(layer))
            if kind == 0:
                h, q, f, i, og, *post_w = pre((d, d, d, d), (F32, F32, F32, F32))
                mix = _hgrn2_call(q, f, i, og, hg_lower_bounds, hg_g_norm[j][None, :], layer)
            elif kind == 1:
                dqk = (ml_w_qkvo.shape[2] - 2 * d) // 2
                h, q, k, v, og, gates, gates_t, *post_w = pre((dqk, dqk, d, d), (F32, F32, F32, F32),
                                                              gate_w=ml_w_if[j], gate_b=ml_b_if[j])
                mix = _mlstm_call(q, k, v, og, gates, gates_t, ml_norm[j][None, :])
            else:
                nq = SW_HEADS * SW_HD
                nk = SW_KV_HEADS * SW_HD
                h, q, k, v, *post_w = pre((nq, nk, nk), (F32, F32, F32))
                mix = _swa_call(q, k, v, sw_q_norm[j][None, :], sw_k_norm[j][None, :], sw_sinks[j][None, :])
            jobs = pre_jobs(layer + 1) if layer + 1 < depth else []
            xs, *pre_w = _post_call(h, mix, p, bi, norm_gains, layer, post_w, cast_jobs=jobs)
            xb = 0
        outs.append(xs)
    return outs[0] if batch == 1 else jnp.concatenate(outs, axis=0)
```

```python
import functools

import numpy as np
import jax
import jax.numpy as jnp
from jax import lax
from jax.experimental import pallas as pl
from jax.experimental.pallas import tpu as pltpu

F32 = jnp.float32
BF16 = jnp.bfloat16

EPS = 1e-6
N_MIXERS = 3
HG_HEADS = 8
ML_HEADS = 4
ML_GATE_CAP = 15.0
SW_HEADS = 16
SW_KV_HEADS = 4
SW_HD = 64
SW_WINDOW = 128

V7X_VMEM_LIMIT_BYTES = 56 * 1024 * 1024
LANES = 128
BF16_SUBLANES = 16

ROW_TILE = 512
FF_CHUNK = 256
HG_CHUNK = 128
HG_TILE = 256
ML_CHUNK = 256
SW_BLOCK = 128
SW_TILE = 256


def _dot(a, b):
    return jnp.dot(a, b, preferred_element_type=F32)


def _dot_nt(a, b):
    return lax.dot_general(a, b, (((1,), (1,)), ((), ())), preferred_element_type=F32)


def _dot_tn(a, b):
    return lax.dot_general(a, b, (((0,), (0,)), ((), ())), preferred_element_type=F32)


def _sigmoid(x):
    return 0.5 * jnp.tanh(0.5 * x) + 0.5


def _log_sigmoid(x):
    return jnp.minimum(x, 0.0) - jnp.log1p(jnp.exp(-jnp.abs(x)))


def _rms(x, gain):
    return x * lax.rsqrt(jnp.mean(x * x, axis=-1, keepdims=True) + EPS) * gain


def _split_bf16(x, terms):
    parts = []
    r = x
    for _ in range(terms):
        p = r.astype(BF16)
        parts.append(p)
        r = r - p.astype(F32)
    return parts


def _swiglu_residual(x, gain, wgu_ref, wd_ref):
    d_ff = wd_ref.shape[0]
    xn = _rms(x, gain).astype(BF16)
    y = jnp.zeros_like(x)
    for j in range(d_ff // FF_CHUNK):
        lo = j * FF_CHUNK
        g = _dot(xn, wgu_ref[:, lo:lo + FF_CHUNK])
        u = _dot(xn, wgu_ref[:, d_ff + lo:d_ff + lo + FF_CHUNK])
        a = (g * _sigmoid(g) * u).astype(BF16)
        y = y + _dot(a, wd_ref[lo:lo + FF_CHUNK, :])
    return x + 0.5 * y


def _resident(shape, lead=()):
    nd = len(shape)
    block = (None,) * len(lead) + tuple(shape[len(lead):])
    index = tuple(lead) + (0,) * (nd - len(lead))
    return pl.BlockSpec(block, lambda *_: index, pipeline_mode=pl.Buffered(1))


def _params(semantics):
    return pltpu.CompilerParams(dimension_semantics=semantics,
                                vmem_limit_bytes=V7X_VMEM_LIMIT_BYTES)


def _cast_specs(jobs, n_steps):
    in_specs, out_specs, out_shapes = [], [], []
    for w, lead in jobs:
        rows, cols = w.shape[len(lead):]
        k = next(k for k in (1, 2, 4, 8) if (rows * k) % (n_steps * BF16_SUBLANES) == 0)
        hb = rows * k // n_steps
        in_specs.append(pl.BlockSpec((None,) * len(lead) + (hb, cols),
                                     lambda i, lead=tuple(lead), k=k: lead + (i // k, 0)))
        out_specs.append(pl.BlockSpec((hb, cols), lambda i, k=k: (i // k, 0)))
        out_shapes.append(jax.ShapeDtypeStruct((rows, cols), BF16))
    return in_specs, out_specs, out_shapes


def _cast_slabs(in_refs, out_refs):
    for src, dst in zip(in_refs, out_refs):
        dst[...] = src[...].astype(dst.dtype)


def _pre_kernel(*refs, splits, n_gate, n_cast):
    n_in = 5 + (2 if n_gate else 0)
    ins, cast_in, outs = refs[:n_in], refs[n_in:n_in + n_cast], refs[n_in + n_cast:]
    x_ref, gains_ref, wgu_ref, wd_ref, win_ref = ins[:5]
    outs, cast_out = outs[:len(outs) - n_cast], outs[len(outs) - n_cast:]
    _cast_slabs(cast_in, cast_out)
    h_ref = outs[0]
    proj_refs = outs[1:1 + len(splits)]
    h = _swiglu_residual(x_ref[...], gains_ref[0:1, :], wgu_ref, wd_ref)
    h_ref[...] = h
    xn = _rms(h, gains_ref[1:2, :])
    xb = xn.astype(BF16)
    off = 0
    for r, n in zip(proj_refs, splits):
        r[...] = _dot(xb, win_ref[:, off:off + n]).astype(r.dtype)
        off += n
    if n_gate:
        wif_ref, bif_ref = ins[5:7]
        g_ref, gt_ref = outs[1 + len(splits):]
        xh, xl = _split_bf16(xn, 2)
        wh, wl = _split_bf16(wif_ref[...], 2)
        pre = _dot(xh, wh) + _dot(xl, wh) + _dot(xh, wl) + bif_ref[...]
        capped = ML_GATE_CAP * jnp.tanh(pre / ML_GATE_CAP)
        lane = lax.broadcasted_iota(jnp.int32, capped.shape, 1)
        gates = jnp.where(lane < n_gate // 2, capped, _log_sigmoid(capped))
        g_ref[...] = gates
        gt_ref[...] = gates.T[0:gt_ref.shape[0], :]


def _pre_call(x, bi, gains, layer, weights, splits, out_dtypes, gate_w=None, gate_b=None, cast_jobs=()):
    _, t, d = x.shape
    n_gate = 0 if gate_w is None else gate_w.shape[1]
    tm = ROW_TILE
    row = lambda n: pl.BlockSpec((tm, n), lambda i: (i, 0))
    in_specs = [pl.BlockSpec((None, tm, d), lambda i: (bi, i, 0)), _resident(gains.shape, (layer,))]
    in_specs += [_resident(w.shape) for w in weights]
    args = [x, gains, *weights]
    out_shape = [jax.ShapeDtypeStruct((t, d), F32)]
    out_specs = [row(d)]
    for n, dt in zip(splits, out_dtypes):
        out_shape.append(jax.ShapeDtypeStruct((t, n), dt))
        out_specs.append(row(n))
    if n_gate:
        wpad = jnp.zeros((d, LANES), F32).at[:, :n_gate].set(gate_w)
        bpad = jnp.zeros((1, LANES), F32).at[0, :n_gate].set(gate_b)
        in_specs += [_resident(wpad.shape), _resident(bpad.shape)]
        args += [wpad, bpad]
        out_shape += [jax.ShapeDtypeStruct((t, LANES), F32), jax.ShapeDtypeStruct((n_gate, t), F32)]
        out_specs += [row(LANES), pl.BlockSpec((n_gate, tm), lambda i: (0, i))]
    c_in, c_out, c_shape = _cast_specs(cast_jobs, t // tm)
    return pl.pallas_call(
        functools.partial(_pre_kernel, splits=tuple(splits), n_gate=n_gate, n_cast=len(cast_jobs)),
        grid=(t // tm,), in_specs=in_specs + c_in, out_specs=out_specs + c_out, out_shape=out_shape + c_shape,
        compiler_params=_params(("arbitrary",)), name="pre",
    )(*args, *[w for w, _ in cast_jobs])


def _post_kernel(*refs, n_cast):
    h_ref, o_ref, p_ref, gains_ref, wout_ref, wgu_ref, wd_ref, wpg_ref, wpp_ref = refs[:9]
    cast_in, x_ref, cast_out = refs[9:9 + n_cast], refs[9 + n_cast], refs[10 + n_cast:]
    _cast_slabs(cast_in, cast_out)
    h = h_ref[...] + _dot(o_ref[...], wout_ref[...])
    h = _swiglu_residual(h, gains_ref[2:3, :], wgu_ref, wd_ref)
    gate = _sigmoid(_dot(_rms(h, gains_ref[3:4, :]).astype(BF16), wpg_ref[...]))
    x_ref[...] = h + gate * _dot(p_ref[...].astype(BF16), wpp_ref[...])


def _post_call(h, o, p, bi, gains, layer, weights, cast_jobs=()):
    t, d = h.shape
    tm = ROW_TILE
    row = lambda n: pl.BlockSpec((tm, n), lambda i: (i, 0))
    c_in, c_out, c_shape = _cast_specs(cast_jobs, t // tm)
    return pl.pallas_call(
        functools.partial(_post_kernel, n_cast=len(cast_jobs)), grid=(t // tm,),
        in_specs=[row(d), row(o.shape[1]), pl.BlockSpec((None, None, tm, p.shape[3]), lambda i: (layer, bi, i, 0)),
                  _resident(gains.shape, (layer,))] + [_resident(w.shape) for w in weights] + c_in,
        out_specs=[pl.BlockSpec((None, tm, d), lambda i: (0, i, 0))] + c_out,
        out_shape=[jax.ShapeDtypeStruct((1, t, d), F32)] + c_shape,
        compiler_params=_params(("arbitrary",)), name="post",
    )(h, o, p, gains, *weights, *[w for w, _ in cast_jobs])


def _hgrn2_tables(c):
    n_lev = int(np.log2(c))
    assert 2 ** n_lev == c
    mats = np.zeros((n_lev + 2, c, c), np.float32)
    r = np.arange(c)[:, None]
    j = np.arange(c)[None, :]
    mats[0] = j <= r
    for lev in range(n_lev):
        half = 2 ** lev
        m = (r // (2 * half)) * (2 * half) + half
        mats[1 + lev] = np.where(r >= m, (j >= m) & (j <= r), (j > r) & (j <= m - 1))
    mats[n_lev + 1] = j > r
    x = r ^ j
    lvl = np.where(r > j, np.floor(np.log2(np.maximum(x, 1))).astype(np.int32), -1)
    lvl = np.where(r == j, n_lev, lvl).astype(np.int32)
    mats = mats.reshape(-1, c)
    return jnp.asarray(np.concatenate([mats, mats], axis=1), BF16), jnp.asarray(lvl), n_lev


def _hgrn2_kernel(lbraw_ref, gn_ref, mexp_ref, lvl_ref, q_ref, f_ref, i_ref, og_ref, o_ref, st_ref,
                  *, layer, n_lev, heads, c):
    rows, d = q_ref.shape
    dk = d // heads
    chunks = range(rows // c)
    hh = range(heads)
    rs = {ci: slice(ci * c, (ci + 1) * c) for ci in chunks}
    sl = {h: slice(h * dk, (h + 1) * dk) for h in hh}
    pairs = [(ci, h) for ci in chunks for h in hh]

    @pl.when(pl.program_id(0) == 0)
    def _():
        st_ref[...] = jnp.zeros_like(st_ref)

    lrows = [lbraw_ref[r:r + 1, :] for r in range(lbraw_ref.shape[0])]
    mx = functools.reduce(jnp.maximum, lrows)
    es = [jnp.exp(r - mx) for r in lrows]
    lb = sum(es[1:layer + 1], jnp.zeros_like(mx)) / sum(es)

    key, expo = {}, {}
    for ci in chunks:
        f = f_ref[rs[ci], :]
        t = jnp.exp(-jnp.abs(f))
        r = 1.0 / (1.0 + t)
        pos = f >= 0.0
        tiny = jnp.logical_and(lb <= 0.0, jnp.logical_not(pos))
        num = jnp.where(pos, 1.0 + lb * t, lb + t)
        logf = jnp.log(jnp.where(tiny, 1.0, num) * r) + jnp.where(tiny, f, 0.0)
        key[ci] = (1.0 - lb) * jnp.where(pos, t * r, r)
        expo[ci] = _dot(mexp_ref[...], jnp.concatenate(_split_bf16(logf, 2), axis=0))

    lvl = lvl_ref[...]
    row = lax.broadcasted_iota(jnp.int32, (c, 1), 0)
    qh, kh, qb, kb, vh, bh, att, o = {}, {}, {}, {}, {}, {}, {}, {}
    for ci, h in pairs:
        qv = q_ref[rs[ci], sl[h]]
        qh[ci, h] = qv * _sigmoid(qv)
        kh[ci, h] = key[ci][:, sl[h]]
        qb[ci, h], kb[ci, h] = qh[ci, h].astype(BF16), kh[ci, h].astype(BF16)
        vh[ci, h] = i_ref[rs[ci], sl[h]].astype(BF16)
        bh[ci, h] = expo[ci][0:c, sl[h]]
    for ci, h in pairs:
        att[ci, h] = jnp.where(lvl == n_lev, _dot_nt(qb[ci, h], kb[ci, h]), 0.0)
    for lev in range(n_lev):
        upper = (row & (1 << lev)) != 0
        for ci, h in pairs:
            e = jnp.exp(expo[ci][(1 + lev) * c:(2 + lev) * c, sl[h]]).astype(BF16)
            x = jnp.where(upper, qb[ci, h], kb[ci, h]) * e
            att[ci, h] = jnp.where(lvl == lev, _dot_nt(x, x), att[ci, h])
    st = {h: st_ref[h] for h in hh}
    for ci in chunks:
        for h in hh:
            o[ci, h] = (_dot_nt((qh[ci, h] * jnp.exp(bh[ci, h])).astype(BF16), st[h].astype(BF16))
                        + _dot(att[ci, h].astype(BF16), vh[ci, h]))
        for h in hh:
            khat = (kh[ci, h] * jnp.exp(expo[ci][(n_lev + 1) * c:(n_lev + 2) * c, sl[h]])).astype(BF16)
            st[h] = st[h] * jnp.exp(bh[ci, h][c - 1:c, :]) + _dot_tn(vh[ci, h], khat)
    for h in hh:
        st_ref[h] = st[h]
    for ci, h in pairs:
        ogv = og_ref[rs[ci], sl[h]]
        o_ref[rs[ci], sl[h]] = (_rms(o[ci, h], gn_ref[...]) * (ogv * _sigmoid(ogv))).astype(o_ref.dtype)


def _hgrn2_call(q, f, i, og, lbraw, g_norm, layer):
    t, d = q.shape
    c = HG_CHUNK
    tile = HG_TILE
    mexp, lvl, n_lev = _hgrn2_tables(c)
    dk = d // HG_HEADS
    blk = pl.BlockSpec((tile, d), lambda n: (n, 0))
    return pl.pallas_call(
        functools.partial(_hgrn2_kernel, layer=layer, n_lev=n_lev, heads=HG_HEADS, c=c),
        grid=(t // tile,),
        in_specs=[_resident(lbraw.shape), _resident(g_norm.shape), _resident(mexp.shape),
                  _resident(lvl.shape), blk, blk, blk, blk],
        out_specs=blk, out_shape=jax.ShapeDtypeStruct((t, d), BF16),
        scratch_shapes=[pltpu.VMEM((HG_HEADS, dk, dk), F32)],
        compiler_params=_params(("arbitrary",)), name="hgrn2",
    )(lbraw, g_norm, mexp, lvl, q, f, i, og)


def _mlstm_kernel(tril_ref, norm_ref, q_ref, k_ref, v_ref, og_ref, g_ref, gt_ref, o_ref,
                  c_ref, n_ref, m_ref, *, heads):
    c = q_ref.shape[0]
    dqk = q_ref.shape[1] // heads
    dv = v_ref.shape[1] // heads

    @pl.when(pl.program_id(0) == 0)
    def _():
        c_ref[...] = jnp.zeros_like(c_ref)
        n_ref[...] = jnp.zeros_like(n_ref)
        m_ref[...] = jnp.zeros_like(m_ref)

    tril = tril_ref[...]
    gates = g_ref[...]
    gates_t = gt_ref[...]
    cum_col = sum(_dot(tril, part) for part in _split_bf16(gates, 3))
    cum_row = sum(_dot_nt(part, tril) for part in _split_bf16(gates_t, 3))
    causal = lax.broadcasted_iota(jnp.int32, (c, c), 0) >= lax.broadcasted_iota(jnp.int32, (c, c), 1)

    hh = range(heads)
    ones = jnp.ones((c, LANES), BF16)
    wide = lambda a, n: jnp.concatenate([a] * (n // LANES), axis=1)
    b_col = {h: cum_col[:, heads + h:heads + h + 1] for h in hh}
    b_row = {h: cum_row[heads + h:heads + h + 1, :] for h in hh}
    i_col = {h: gates[:, h:h + 1] for h in hh}
    i_row = {h: gates_t[h:h + 1, :] for h in hh}
    m_prev = {h: m_ref[h:h + 1, 0:1] for h in hh}
    qb = {h: q_ref[:, h * dqk:(h + 1) * dqk].astype(BF16) for h in hh}
    kh = {h: k_ref[:, h * dqk:(h + 1) * dqk] * (dqk ** -0.5) for h in hh}
    vh = {h: v_ref[:, h * dv:(h + 1) * dv].astype(BF16) for h in hh}
    cst = {h: c_ref[h] for h in hh}
    nst = {h: n_ref[h] for h in hh}

    b_rep = {h: jnp.broadcast_to(b_col[h], (c, LANES)) for h in hh}
    log_intra = {h: jnp.where(causal, wide(b_rep[h], c) - (b_row[h] - i_row[h]), -jnp.inf) for h in hh}
    m_col = {h: jnp.maximum(jnp.max(log_intra[h], axis=1, keepdims=True), b_col[h] + m_prev[h]) for h in hh}
    m_rep = {h: jnp.broadcast_to(m_col[h], (c, LANES)) for h in hh}
    w_intra = {h: jnp.exp(log_intra[h] - wide(m_rep[h], c)) for h in hh}
    w_inter = {h: jnp.exp(b_rep[h] + m_prev[h] - m_rep[h]) for h in hh}
    scores = {h: (_dot_nt(qb[h], kh[h].astype(BF16)) * w_intra[h]).astype(BF16) for h in hh}
    inter = {h: _dot(qb[h], cst[h].astype(BF16)) for h in hh}
    num = {h: wide(w_inter[h], dv) * inter[h] + _dot(scores[h], vh[h]) for h in hh}
    den = {h: w_inter[h] * _dot(qb[h], nst[h].astype(BF16)) + _dot(scores[h], ones) for h in hh}
    inv = {h: 1.0 / jnp.maximum(jnp.abs(den[h]), jnp.exp(-m_rep[h])) for h in hh}
    hid = {h: num[h] * wide(inv[h], dv) for h in hh}

    b_last = {h: b_col[h][c - 1:c, :] for h in hh}
    log_state = {h: b_last[h] + m_prev[h] for h in hh}
    m_new = {h: jnp.maximum(log_state[h], jnp.max(b_last[h] - b_row[h] + i_row[h], axis=1, keepdims=True))
             for h in hh}
    w_src = {h: jnp.broadcast_to(jnp.exp(b_last[h] - b_col[h] + i_col[h] - m_new[h]), (c, LANES)) for h in hh}
    decay = {h: jnp.exp(log_state[h] - m_new[h]) for h in hh}
    khat = {h: (kh[h] * w_src[h]).astype(BF16) for h in hh}
    for h in hh:
        c_ref[h] = decay[h] * cst[h] + _dot_tn(khat[h], vh[h])
        n_ref[h] = decay[h] * nst[h] + _dot_tn(khat[h], ones)
        m_ref[h:h + 1, :] = jnp.broadcast_to(m_new[h], (1, m_ref.shape[1]))
    ones_dv = jnp.ones((dv, LANES), BF16)
    for h in hh:
        sq_hi, sq_lo = _split_bf16(hid[h] * hid[h], 2)
        ms = (_dot(sq_hi, ones_dv) + _dot(sq_lo, ones_dv)) * (1.0 / dv)
        ogv = og_ref[:, h * dv:(h + 1) * dv]
        o_ref[:, h * dv:(h + 1) * dv] = (hid[h] * wide(lax.rsqrt(ms + EPS), dv) * norm_ref[:, h * dv:(h + 1) * dv]
                                         * _sigmoid(ogv)).astype(o_ref.dtype)


def _mlstm_call(q, k, v, og, gates, gates_t, norm):
    t = q.shape[0]
    d = v.shape[1]
    c = ML_CHUNK
    dqk = q.shape[1] // ML_HEADS
    dv = d // ML_HEADS
    tril = jnp.asarray(np.tril(np.ones((c, c), np.float32)), BF16)
    blk = lambda n: pl.BlockSpec((c, n), lambda s: (s, 0))
    return pl.pallas_call(
        functools.partial(_mlstm_kernel, heads=ML_HEADS),
        grid=(t // c,),
        in_specs=[_resident(tril.shape), _resident(norm.shape), blk(q.shape[1]), blk(k.shape[1]),
                  blk(d), blk(d), blk(LANES), pl.BlockSpec((2 * ML_HEADS, c), lambda s: (0, s))],
        out_specs=blk(d), out_shape=jax.ShapeDtypeStruct((t, d), BF16),
        scratch_shapes=[pltpu.VMEM((ML_HEADS, dqk, dv), F32), pltpu.VMEM((ML_HEADS, dqk, LANES), F32),
                        pltpu.VMEM((8, LANES), F32)],
        compiler_params=_params(("arbitrary",)), name="mlstm",
    )(tril, norm, q, k, v, og, gates, gates_t)


def _swa_bias(blk):
    group = SW_HEADS // SW_KV_HEADS
    kpos = np.arange(2 * blk)[:, None] - blk
    dist = np.arange(blk)[None, :] - kpos
    slopes = 2.0 ** (-8.0 * (np.arange(SW_HEADS) + 1) / SW_HEADS)
    ok = (dist >= 0) & (dist < SW_WINDOW)
    tables = []
    for first in (True, False):
        valid = ok & ((kpos >= 0) | (not first))
        bias = np.where(valid[None], -slopes[:, None, None] * dist[None], -np.inf)
        bias = bias.reshape(SW_KV_HEADS, group, 2 * blk, blk).transpose(0, 2, 1, 3)
        tables.append(bias.reshape(SW_KV_HEADS, 2 * blk, group * blk))
    return jnp.asarray(np.stack(tables), F32)


def _group_mean_sq(x, n):
    lanes = x.shape[1]
    same = (lax.broadcasted_iota(jnp.int32, (lanes, lanes), 0) // n
            == lax.broadcasted_iota(jnp.int32, (lanes, lanes), 1) // n)
    ones = jnp.where(same, 1.0, 0.0).astype(BF16)
    hi, lo = _split_bf16(x * x, 2)
    return (_dot(hi, ones) + _dot(lo, ones)) * (1.0 / n)


def _swa_kernel(qg_ref, kg_ref, sink_ref, bias_ref, q_ref, kp_ref, kc_ref, vp_ref, vc_ref, o_ref, *, blk):
    group = SW_HEADS // SW_KV_HEADS
    n_blk = q_ref.shape[0] // blk
    lo = lax.broadcasted_iota(jnp.int32, (1, LANES), 1) < SW_HD
    kk = jnp.concatenate([kp_ref[...], kc_ref[...]], axis=0)
    vv = jnp.concatenate([vp_ref[...], vc_ref[...]], axis=0)
    q_scale = qg_ref[...] * (SW_HD ** -0.5)
    kn, v_low, v_high, qst = {}, {}, {}, {}
    for kg in range(SW_KV_HEADS // 2):
        kx = kk[:, kg * LANES:(kg + 1) * LANES]
        vx = vv[:, kg * LANES:(kg + 1) * LANES]
        kx = kx * lax.rsqrt(_group_mean_sq(kx, SW_HD) + EPS) * kg_ref[...]
        kr = pltpu.roll(kx, SW_HD, axis=1)
        vr = pltpu.roll(vx, SW_HD, axis=1)
        for half in range(2):
            kv = 2 * kg + half
            kn[kv] = (jnp.where(lo, kx, kr) if half == 0 else jnp.where(lo, kr, kx)).astype(BF16)
            v_low[kv] = jnp.where(lo, vx if half == 0 else vr, 0.0).astype(BF16)
            v_high[kv] = jnp.where(lo, 0.0, vr if half == 0 else vx).astype(BF16)
    pairs = [(b, kv) for b in range(n_blk) for kv in range(SW_KV_HEADS)]
    for b, kv in pairs:
        qs = []
        for j in range(kv * group // 2, (kv + 1) * group // 2):
            qx = q_ref[b * blk:(b + 1) * blk, j * LANES:(j + 1) * LANES]
            qn = qx * lax.rsqrt(_group_mean_sq(qx, SW_HD) + EPS) * q_scale
            qs += [jnp.where(lo, qn, 0.0).astype(BF16), jnp.where(lo, 0.0, qn).astype(BF16)]
        qst[b, kv] = jnp.concatenate(qs, axis=0)
    s = {(b, kv): _dot_nt(kn[kv][b * blk:(b + 2) * blk], qst[b, kv]) + bias_ref[0 if b == 0 else 1, kv]
         for b, kv in pairs}
    pn = {}
    for b, kv in pairs:
        sink = sink_ref[:, kv * group * blk:(kv + 1) * group * blk]
        mx = jnp.maximum(jnp.max(s[b, kv], axis=0, keepdims=True), sink)
        p = jnp.exp(s[b, kv] - mx)
        inv = 1.0 / (jnp.sum(p, axis=0, keepdims=True) + jnp.exp(sink - mx))
        pn[b, kv] = (p * inv).astype(BF16)
    for b, kv in pairs:
        ks = slice(b * blk, (b + 2) * blk)
        for idx, j in enumerate(range(kv * group // 2, (kv + 1) * group // 2)):
            even = pn[b, kv][:, 2 * idx * blk:(2 * idx + 1) * blk]
            odd = pn[b, kv][:, (2 * idx + 1) * blk:(2 * idx + 2) * blk]
            o_ref[b * blk:(b + 1) * blk, j * LANES:(j + 1) * LANES] = (
                _dot_tn(even, v_low[kv][ks]) + _dot_tn(odd, v_high[kv][ks])).astype(o_ref.dtype)


def _swa_call(q, k, v, q_gain, k_gain, sinks):
    t, nq = q.shape
    nk = k.shape[1]
    b = SW_BLOCK
    n_blk = SW_TILE // b
    bias = _swa_bias(b)
    qg2 = jnp.tile(q_gain, (1, LANES // SW_HD))
    kg2 = jnp.tile(k_gain, (1, LANES // SW_HD))
    sink_row = jnp.repeat(sinks, b, axis=1)
    cur = lambda n: pl.BlockSpec((SW_TILE, n), lambda s: (s, 0))
    prev = lambda n: pl.BlockSpec((b, n), lambda s: (jnp.maximum(s * n_blk - 1, 0), 0))
    bias2 = jnp.stack([bias, jnp.stack([bias[1], bias[1]])])
    bias_spec = pl.BlockSpec((None,) + bias2.shape[1:], lambda s: (jnp.minimum(s, 1), 0, 0, 0, 0))
    return pl.pallas_call(
        functools.partial(_swa_kernel, blk=b), grid=(t // SW_TILE,),
        in_specs=[_resident(qg2.shape), _resident(kg2.shape), _resident(sink_row.shape), bias_spec,
                  cur(nq), prev(nk), cur(nk), prev(nk), cur(nk)],
        out_specs=cur(nq), out_shape=jax.ShapeDtypeStruct((t, nq), BF16),
        compiler_params=_params(("arbitrary",)), name="swa",
    )(qg2, kg2, sink_row, bias2, q, k, k, v, v)


def kernel(x, p, norm_gains, w_ffn_gu, w_ffn_down, w_ple_gate, w_ple_proj, hg_lower_bounds, hg_w_in,
           hg_g_norm, hg_w_out, ml_w_qkvo, ml_w_if, ml_b_if, ml_norm, ml_w_out, sw_w_qkv, sw_q_norm,
           sw_k_norm, sw_sinks, sw_w_o):
    batch, seq, d = x.shape
    depth = p.shape[0]
    w_in = (hg_w_in, ml_w_qkvo, sw_w_qkv)
    w_out = (hg_w_out, ml_w_out, sw_w_o)

    def pre_jobs(layer):
        return [(w_ffn_gu, (layer, 0)), (w_ffn_down, (layer, 0)), (w_in[layer % N_MIXERS], (layer // N_MIXERS,))]

    def post_jobs(layer):
        return [(w_out[layer % N_MIXERS], (layer // N_MIXERS,)), (w_ffn_gu, (layer, 1)), (w_ffn_down, (layer, 1)),
                (w_ple_gate, (layer,)), (w_ple_proj, (layer,))]

    outs = []
    for bi in range(batch):
        xs, xb = x, bi
        pre_w = [w[lead].astype(BF16) for w, lead in pre_jobs(0)]
        for layer in range(depth):
            kind, j = layer % N_MIXERS, layer // N_MIXERS
            pre = functools.partial(_pre_call, xs, xb, norm_gains, layer, pre_w, cast_jobs=post_jobs(layer))
            if kind == 0:
                h, q, f, i, og, *post_w = pre((d, d, d, d), (F32, F32, F32, F32))
                mix = _hgrn2_call(q, f, i, og, hg_lower_bounds, hg_g_norm[j][None, :], layer)
            elif kind == 1:
                dqk = (ml_w_qkvo.shape[2] - 2 * d) // 2
                h, q, k, v, og, gates, gates_t, *post_w = pre((dqk, dqk, d, d), (F32, F32, F32, F32),
                                                              gate_w=ml_w_if[j], gate_b=ml_b_if[j])
                mix = _mlstm_call(q, k, v, og, gates, gates_t, ml_norm[j][None, :])
            else:
                nq = SW_HEADS * SW_HD
                nk = SW_KV_HEADS * SW_HD
                h, q, k, v, *post_w = pre((nq, nk, nk), (F32, F32, F32))
                mix = _swa_call(q, k, v, sw_q_norm[j][None, :], sw_k_norm[j][None, :], sw_sinks[j][None, :])
            jobs = pre_jobs(layer + 1) if layer + 1 < depth else []
            xs, *pre_w = _post_call(h, mix, p, bi, norm_gains, layer, post_w, cast_jobs=jobs)
            xb = 0
        outs.append(xs)
    return outs[0] if batch == 1 else jnp.concatenate(outs, axis=0)
```

```python
import functools

import numpy as np
import jax
import jax.numpy as jnp
from jax import lax
from jax.experimental import pallas as pl
from jax.experimental.pallas import tpu as pltpu

F32 = jnp.float32
BF16 = jnp.bfloat16

EPS = 1e-6
N_MIXERS = 3
HG_HEADS = 8
ML_HEADS = 4
ML_GATE_CAP = 15.0
SW_HEADS = 16
SW_KV_HEADS = 4
SW_HD = 64
SW_WINDOW = 128

V7X_VMEM_LIMIT_BYTES = 56 * 1024 * 1024
LANES = 128
BF16_SUBLANES = 16

ROW_TILE = 512
FF_CHUNK = 256
HG_CHUNK = 128
HG_TILE = 256
ML_CHUNK = 256
SW_BLOCK = 128
SW_TILE = 256


def _dot(a, b):
    return jnp.dot(a, b, preferred_element_type=F32)


def _dot_nt(a, b):
    return lax.dot_general(a, b, (((1,), (1,)), ((), ())), preferred_element_type=F32)


def _dot_tn(a, b):
    return lax.dot_general(a, b, (((0,), (0,)), ((), ())), preferred_element_type=F32)


def _sigmoid(x):
    return 0.5 * jnp.tanh(0.5 * x) + 0.5


def _log_sigmoid(x):
    return jnp.minimum(x, 0.0) - jnp.log1p(jnp.exp(-jnp.abs(x)))


def _rms(x, gain):
    return x * lax.rsqrt(jnp.mean(x * x, axis=-1, keepdims=True) + EPS) * gain


def _split_bf16(x, terms):
    parts = []
    r = x
    for _ in range(terms):
        p = r.astype(BF16)
        parts.append(p)
        r = r - p.astype(F32)
    return parts


def _swiglu_residual(x, gain, wg_ref, wu_ref, wd_ref):
    d_ff = wd_ref.shape[0]
    xn = _rms(x, gain).astype(BF16)
    y = jnp.zeros_like(x)
    for j in range(d_ff // FF_CHUNK):
        lo = j * FF_CHUNK
        g = _dot(xn, wg_ref[:, lo:lo + FF_CHUNK])
        u = _dot(xn, wu_ref[:, lo:lo + FF_CHUNK])
        a = (g * _sigmoid(g) * u).astype(BF16)
        y = y + _dot(a, wd_ref[lo:lo + FF_CHUNK, :])
    return x + 0.5 * y


def _resident(shape, lead=()):
    nd = len(shape)
    block = (None,) * len(lead) + tuple(shape[len(lead):])
    index = tuple(lead) + (0,) * (nd - len(lead))
    return pl.BlockSpec(block, lambda *_: index, pipeline_mode=pl.Buffered(1))


def _resident_cols(shape, width, index):
    return pl.BlockSpec((shape[0], width), lambda *_: (0, index), pipeline_mode=pl.Buffered(1))


def _params(semantics):
    return pltpu.CompilerParams(dimension_semantics=semantics,
                                vmem_limit_bytes=V7X_VMEM_LIMIT_BYTES)


def _cast_specs(jobs, n_steps):
    in_specs, out_specs, out_shapes = [], [], []
    for w, lead in jobs:
        rows, cols = w.shape[len(lead):]
        k = next(k for k in (1, 2, 4, 8) if (rows * k) % (n_steps * BF16_SUBLANES) == 0)
        hb = rows * k // n_steps
        in_specs.append(pl.BlockSpec((None,) * len(lead) + (hb, cols),
                                     lambda i, lead=tuple(lead), k=k: lead + (i // k, 0)))
        out_specs.append(pl.BlockSpec((hb, cols), lambda i, k=k: (i // k, 0)))
        out_shapes.append(jax.ShapeDtypeStruct((rows, cols), BF16))
    return in_specs, out_specs, out_shapes


def _cast_slabs(in_refs, out_refs):
    for src, dst in zip(in_refs, out_refs):
        dst[...] = src[...].astype(dst.dtype)


def _pre_kernel(*refs, splits, n_gate, n_cast):
    n_w = 5 + len(splits)
    n_in = n_w + (2 if n_gate else 0)
    ins, cast_in, outs = refs[:n_in], refs[n_in:n_in + n_cast], refs[n_in + n_cast:]
    x_ref, gains_ref, wg_ref, wu_ref, wd_ref = ins[:5]
    win_refs = ins[5:n_w]
    outs, cast_out = outs[:len(outs) - n_cast], outs[len(outs) - n_cast:]
    _cast_slabs(cast_in, cast_out)
    h_ref = outs[0]
    proj_refs = outs[1:1 + len(splits)]
    h = _swiglu_residual(x_ref[...], gains_ref[0:1, :], wg_ref, wu_ref, wd_ref)
    h_ref[...] = h
    xn = _rms(h, gains_ref[1:2, :])
    xb = xn.astype(BF16)
    for r, w_ref in zip(proj_refs, win_refs):
        r[...] = _dot(xb, w_ref[...]).astype(r.dtype)
    if n_gate:
        wif_ref, bif_ref = ins[n_w:n_w + 2]
        g_ref, gt_ref = outs[1 + len(splits):]
        xh, xl = _split_bf16(xn, 2)
        wh, wl = _split_bf16(wif_ref[...], 2)
        pre = _dot(xh, wh) + _dot(xl, wh) + _dot(xh, wl) + bif_ref[...]
        capped = ML_GATE_CAP * jnp.tanh(pre / ML_GATE_CAP)
        lane = lax.broadcasted_iota(jnp.int32, capped.shape, 1)
        gates = jnp.where(lane < n_gate // 2, capped, _log_sigmoid(capped))
        g_ref[...] = gates
        gt_ref[...] = gates.T[0:gt_ref.shape[0], :]


def _pre_call(x, bi, gains, layer, weights, splits, out_dtypes, gate_w=None, gate_b=None, cast_jobs=()):
    _, t, d = x.shape
    n_gate = 0 if gate_w is None else gate_w.shape[1]
    tm = ROW_TILE
    row = lambda n: pl.BlockSpec((tm, n), lambda i: (i, 0))
    wgu, wd, win = weights
    d_ff = wd.shape[0]
    offs = np.cumsum([0] + list(splits))[:-1]
    in_specs = [pl.BlockSpec((None, tm, d), lambda i: (bi, i, 0)), _resident(gains.shape, (layer,)),
                _resident_cols(wgu.shape, d_ff, 0), _resident_cols(wgu.shape, d_ff, 1), _resident(wd.shape)]
    in_specs += [_resident_cols(win.shape, n, int(off) // n) for n, off in zip(splits, offs)]
    args = [x, gains, wgu, wgu, wd] + [win] * len(splits)
    out_shape = [jax.ShapeDtypeStruct((t, d), F32)]
    out_specs = [row(d)]
    for n, dt in zip(splits, out_dtypes):
        out_shape.append(jax.ShapeDtypeStruct((t, n), dt))
        out_specs.append(row(n))
    if n_gate:
        wpad = jnp.zeros((d, LANES), F32).at[:, :n_gate].set(gate_w)
        bpad = jnp.zeros((1, LANES), F32).at[0, :n_gate].set(gate_b)
        in_specs += [_resident(wpad.shape), _resident(bpad.shape)]
        args += [wpad, bpad]
        out_shape += [jax.ShapeDtypeStruct((t, LANES), F32), jax.ShapeDtypeStruct((n_gate, t), F32)]
        out_specs += [row(LANES), pl.BlockSpec((n_gate, tm), lambda i: (0, i))]
    c_in, c_out, c_shape = _cast_specs(cast_jobs, t // tm)
    return pl.pallas_call(
        functools.partial(_pre_kernel, splits=tuple(splits), n_gate=n_gate, n_cast=len(cast_jobs)),
        grid=(t // tm,), in_specs=in_specs + c_in, out_specs=out_specs + c_out, out_shape=out_shape + c_shape,
        compiler_params=_params(("arbitrary",)), name="pre",
    )(*args, *[w for w, _ in cast_jobs])


def _post_kernel(*refs, n_cast):
    h_ref, o_ref, p_ref, gains_ref, wout_ref, wg_ref, wu_ref, wd_ref, wpg_ref, wpp_ref = refs[:10]
    cast_in, x_ref, cast_out = refs[10:10 + n_cast], refs[10 + n_cast], refs[11 + n_cast:]
    _cast_slabs(cast_in, cast_out)
    h = h_ref[...] + _dot(o_ref[...], wout_ref[...])
    h = _swiglu_residual(h, gains_ref[2:3, :], wg_ref, wu_ref, wd_ref)
    gate = _sigmoid(_dot(_rms(h, gains_ref[3:4, :]).astype(BF16), wpg_ref[...]))
    x_ref[...] = h + gate * _dot(p_ref[...].astype(BF16), wpp_ref[...])


def _post_call(h, o, p, bi, gains, layer, weights, cast_jobs=()):
    t, d = h.shape
    tm = ROW_TILE
    row = lambda n: pl.BlockSpec((tm, n), lambda i: (i, 0))
    c_in, c_out, c_shape = _cast_specs(cast_jobs, t // tm)
    wout, wgu, wd, wpg, wpp = weights
    d_ff = wd.shape[0]
    w_specs = [_resident(wout.shape), _resident_cols(wgu.shape, d_ff, 0), _resident_cols(wgu.shape, d_ff, 1),
               _resident(wd.shape), _resident(wpg.shape), _resident(wpp.shape)]
    return pl.pallas_call(
        functools.partial(_post_kernel, n_cast=len(cast_jobs)), grid=(t // tm,),
        in_specs=[row(d), row(o.shape[1]), pl.BlockSpec((None, None, tm, p.shape[3]), lambda i: (layer, bi, i, 0)),
                  _resident(gains.shape, (layer,))] + w_specs + c_in,
        out_specs=[pl.BlockSpec((None, tm, d), lambda i: (0, i, 0))] + c_out,
        out_shape=[jax.ShapeDtypeStruct((1, t, d), F32)] + c_shape,
        compiler_params=_params(("arbitrary",)), name="post",
    )(h, o, p, gains, wout, wgu, wgu, wd, wpg, wpp, *[w for w, _ in cast_jobs])


def _hgrn2_tables(c):
    n_lev = int(np.log2(c))
    assert 2 ** n_lev == c
    mats = np.zeros((n_lev + 2, c, c), np.float32)
    r = np.arange(c)[:, None]
    j = np.arange(c)[None, :]
    mats[0] = j <= r
    for lev in range(n_lev):
        half = 2 ** lev
        m = (r // (2 * half)) * (2 * half) + half
        mats[1 + lev] = np.where(r >= m, (j >= m) & (j <= r), (j > r) & (j <= m - 1))
    mats[n_lev + 1] = j > r
    x = r ^ j
    lvl = np.where(r > j, np.floor(np.log2(np.maximum(x, 1))).astype(np.int32), -1)
    lvl = np.where(r == j, n_lev, lvl).astype(np.int32)
    mats = mats.reshape(-1, c)
    return jnp.asarray(np.concatenate([mats, mats], axis=1), BF16), jnp.asarray(lvl), n_lev


def _hgrn2_kernel(lbraw_ref, gn_ref, mexp_ref, lvl_ref, q_ref, f_ref, i_ref, og_ref, o_ref, st_ref,
                  *, layer, n_lev, heads, c):
    rows, d = q_ref.shape
    dk = d // heads
    chunks = range(rows // c)
    hh = range(heads)
    rs = {ci: slice(ci * c, (ci + 1) * c) for ci in chunks}
    sl = {h: slice(h * dk, (h + 1) * dk) for h in hh}
    pairs = [(ci, h) for ci in chunks for h in hh]

    @pl.when(pl.program_id(0) == 0)
    def _():
        st_ref[...] = jnp.zeros_like(st_ref)

    lrows = [lbraw_ref[r:r + 1, :] for r in range(lbraw_ref.shape[0])]
    mx = functools.reduce(jnp.maximum, lrows)
    es = [jnp.exp(r - mx) for r in lrows]
    lb = sum(es[1:layer + 1], jnp.zeros_like(mx)) / sum(es)

    key, expo = {}, {}
    for ci in chunks:
        f = f_ref[rs[ci], :]
        t = jnp.exp(-jnp.abs(f))
        r = 1.0 / (1.0 + t)
        pos = f >= 0.0
        tiny = jnp.logical_and(lb <= 0.0, jnp.logical_not(pos))
        num = jnp.where(pos, 1.0 + lb * t, lb + t)
        logf = jnp.log(jnp.where(tiny, 1.0, num) * r) + jnp.where(tiny, f, 0.0)
        key[ci] = (1.0 - lb) * jnp.where(pos, t * r, r)
        expo[ci] = _dot(mexp_ref[...], jnp.concatenate(_split_bf16(logf, 2), axis=0))

    lvl = lvl_ref[...]
    row = lax.broadcasted_iota(jnp.int32, (c, 1), 0)
    qh, kh, qb, kb, vh, bh, att, o = {}, {}, {}, {}, {}, {}, {}, {}
    for ci, h in pairs:
        qv = q_ref[rs[ci], sl[h]]
        qh[ci, h] = qv * _sigmoid(qv)
        kh[ci, h] = key[ci][:, sl[h]]
        qb[ci, h], kb[ci, h] = qh[ci, h].astype(BF16), kh[ci, h].astype(BF16)
        vh[ci, h] = i_ref[rs[ci], sl[h]].astype(BF16)
        bh[ci, h] = expo[ci][0:c, sl[h]]
    for ci, h in pairs:
        att[ci, h] = jnp.where(lvl == n_lev, _dot_nt(qb[ci, h], kb[ci, h]), 0.0)
    for lev in range(n_lev):
        upper = (row & (1 << lev)) != 0
        for ci, h in pairs:
            e = jnp.exp(expo[ci][(1 + lev) * c:(2 + lev) * c, sl[h]]).astype(BF16)
            x = jnp.where(upper, qb[ci, h], kb[ci, h]) * e
            att[ci, h] = jnp.where(lvl == lev, _dot_nt(x, x), att[ci, h])
    st = {h: st_ref[h] for h in hh}
    for ci in chunks:
        for h in hh:
            o[ci, h] = (_dot_nt((qh[ci, h] * jnp.exp(bh[ci, h])).astype(BF16), st[h].astype(BF16))
                        + _dot(att[ci, h].astype(BF16), vh[ci, h]))
        for h in hh:
            khat = (kh[ci, h] * jnp.exp(expo[ci][(n_lev + 1) * c:(n_lev + 2) * c, sl[h]])).astype(BF16)
            st[h] = st[h] * jnp.exp(bh[ci, h][c - 1:c, :]) + _dot_tn(vh[ci, h], khat)
    for h in hh:
        st_ref[h] = st[h]
    for ci, h in pairs:
        ogv = og_ref[rs[ci], sl[h]]
        o_ref[rs[ci], sl[h]] = (_rms(o[ci, h], gn_ref[...]) * (ogv * _sigmoid(ogv))).astype(o_ref.dtype)


def _hgrn2_call(q, f, i, og, lbraw, g_norm, layer):
    t, d = q.shape
    c = HG_CHUNK
    tile = HG_TILE
    mexp, lvl, n_lev = _hgrn2_tables(c)
    dk = d // HG_HEADS
    blk = pl.BlockSpec((tile, d), lambda n: (n, 0))
    return pl.pallas_call(
        functools.partial(_hgrn2_kernel, layer=layer, n_lev=n_lev, heads=HG_HEADS, c=c),
        grid=(t // tile,),
        in_specs=[_resident(lbraw.shape), _resident(g_norm.shape), _resident(mexp.shape),
                  _resident(lvl.shape), blk, blk, blk, blk],
        out_specs=blk, out_shape=jax.ShapeDtypeStruct((t, d), BF16),
        scratch_shapes=[pltpu.VMEM((HG_HEADS, dk, dk), F32)],
        compiler_params=_params(("arbitrary",)), name="hgrn2",
    )(lbraw, g_norm, mexp, lvl, q, f, i, og)


def _mlstm_kernel(tril_ref, norm_ref, q_ref, k_ref, v_ref, og_ref, g_ref, gt_ref, o_ref,
                  c_ref, n_ref, m_ref, *, heads):
    c = q_ref.shape[0]
    dqk = q_ref.shape[1] // heads
    dv = v_ref.shape[1] // heads

    @pl.when(pl.program_id(0) == 0)
    def _():
        c_ref[...] = jnp.zeros_like(c_ref)
        n_ref[...] = jnp.zeros_like(n_ref)
        m_ref[...] = jnp.zeros_like(m_ref)

    tril = tril_ref[...]
    gates = g_ref[...]
    gates_t = gt_ref[...]
    cum_col = sum(_dot(tril, part) for part in _split_bf16(gates, 3))
    cum_row = sum(_dot_nt(part, tril) for part in _split_bf16(gates_t, 3))
    causal = lax.broadcasted_iota(jnp.int32, (c, c), 0) >= lax.broadcasted_iota(jnp.int32, (c, c), 1)

    hh = range(heads)
    ones = jnp.ones((c, LANES), BF16)
    wide = lambda a, n: jnp.concatenate([a] * (n // LANES), axis=1)
    b_col = {h: cum_col[:, heads + h:heads + h + 1] for h in hh}
    b_row = {h: cum_row[heads + h:heads + h + 1, :] for h in hh}
    i_col = {h: gates[:, h:h + 1] for h in hh}
    i_row = {h: gates_t[h:h + 1, :] for h in hh}
    m_prev = {h: m_ref[h:h + 1, 0:1] for h in hh}
    qb = {h: q_ref[:, h * dqk:(h + 1) * dqk].astype(BF16) for h in hh}
    kh = {h: k_ref[:, h * dqk:(h + 1) * dqk] * (dqk ** -0.5) for h in hh}
    vh = {h: v_ref[:, h * dv:(h + 1) * dv].astype(BF16) for h in hh}
    cst = {h: c_ref[h] for h in hh}
    nst = {h: n_ref[h] for h in hh}

    b_rep = {h: jnp.broadcast_to(b_col[h], (c, LANES)) for h in hh}
    log_intra = {h: jnp.where(causal, wide(b_rep[h], c) - (b_row[h] - i_row[h]), -jnp.inf) for h in hh}
    m_col = {h: jnp.maximum(jnp.max(log_intra[h], axis=1, keepdims=True), b_col[h] + m_prev[h]) for h in hh}
    m_rep = {h: jnp.broadcast_to(m_col[h], (c, LANES)) for h in hh}
    w_intra = {h: jnp.exp(log_intra[h] - wide(m_rep[h], c)) for h in hh}
    w_inter = {h: jnp.exp(b_rep[h] + m_prev[h] - m_rep[h]) for h in hh}
    scores = {h: (_dot_nt(qb[h], kh[h].astype(BF16)) * w_intra[h]).astype(BF16) for h in hh}
    inter = {h: _dot(qb[h], cst[h].astype(BF16)) for h in hh}
    num = {h: wide(w_inter[h], dv) * inter[h] + _dot(scores[h], vh[h]) for h in hh}
    den = {h: w_inter[h] * _dot(qb[h], nst[h].astype(BF16)) + _dot(scores[h], ones) for h in hh}
    inv = {h: 1.0 / jnp.maximum(jnp.abs(den[h]), jnp.exp(-m_rep[h])) for h in hh}
    hid = {h: num[h] * wide(inv[h], dv) for h in hh}

    b_last = {h: b_col[h][c - 1:c, :] for h in hh}
    log_state = {h: b_last[h] + m_prev[h] for h in hh}
    m_new = {h: jnp.maximum(log_state[h], jnp.max(b_last[h] - b_row[h] + i_row[h], axis=1, keepdims=True))
             for h in hh}
    w_src = {h: jnp.broadcast_to(jnp.exp(b_last[h] - b_col[h] + i_col[h] - m_new[h]), (c, LANES)) for h in hh}
    decay = {h: jnp.exp(log_state[h] - m_new[h]) for h in hh}
    khat = {h: (kh[h] * w_src[h]).astype(BF16) for h in hh}
    for h in hh:
        c_ref[h] = decay[h] * cst[h] + _dot_tn(khat[h], vh[h])
        n_ref[h] = decay[h] * nst[h] + _dot_tn(khat[h], ones)
        m_ref[h:h + 1, :] = jnp.broadcast_to(m_new[h], (1, m_ref.shape[1]))
    ones_dv = jnp.ones((dv, LANES), BF16)
    for h in hh:
        sq_hi, sq_lo = _split_bf16(hid[h] * hid[h], 2)
        ms = (_dot(sq_hi, ones_dv) + _dot(sq_lo, ones_dv)) * (1.0 / dv)
        ogv = og_ref[:, h * dv:(h + 1) * dv]
        o_ref[:, h * dv:(h + 1) * dv] = (hid[h] * wide(lax.rsqrt(ms + EPS), dv) * norm_ref[:, h * dv:(h + 1) * dv]
                                         * _sigmoid(ogv)).astype(o_ref.dtype)


def _mlstm_call(q, k, v, og, gates, gates_t, norm):
    t = q.shape[0]
    d = v.shape[1]
    c = ML_CHUNK
    dqk = q.shape[1] // ML_HEADS
    dv = d // ML_HEADS
    tril = jnp.asarray(np.tril(np.ones((c, c), np.float32)), BF16)
    blk = lambda n: pl.BlockSpec((c, n), lambda s: (s, 0))
    return pl.pallas_call(
        functools.partial(_mlstm_kernel, heads=ML_HEADS),
        grid=(t // c,),
        in_specs=[_resident(tril.shape), _resident(norm.shape), blk(q.shape[1]), blk(k.shape[1]),
                  blk(d), blk(d), blk(LANES), pl.BlockSpec((2 * ML_HEADS, c), lambda s: (0, s))],
        out_specs=blk(d), out_shape=jax.ShapeDtypeStruct((t, d), BF16),
        scratch_shapes=[pltpu.VMEM((ML_HEADS, dqk, dv), F32), pltpu.VMEM((ML_HEADS, dqk, LANES), F32),
                        pltpu.VMEM((8, LANES), F32)],
        compiler_params=_params(("arbitrary",)), name="mlstm",
    )(tril, norm, q, k, v, og, gates, gates_t)


def _swa_bias(blk):
    group = SW_HEADS // SW_KV_HEADS
    kpos = np.arange(2 * blk)[:, None] - blk
    dist = np.arange(blk)[None, :] - kpos
    slopes = 2.0 ** (-8.0 * (np.arange(SW_HEADS) + 1) / SW_HEADS)
    ok = (dist >= 0) & (dist < SW_WINDOW)
    tables = []
    for first in (True, False):
        valid = ok & ((kpos >= 0) | (not first))
        bias = np.where(valid[None], -slopes[:, None, None] * dist[None], -np.inf)
        bias = bias.reshape(SW_KV_HEADS, group, 2 * blk, blk).transpose(0, 2, 1, 3)
        tables.append(bias.reshape(SW_KV_HEADS, 2 * blk, group * blk))
    return jnp.asarray(np.stack(tables), F32)


def _group_mean_sq(x, n):
    lanes = x.shape[1]
    same = (lax.broadcasted_iota(jnp.int32, (lanes, lanes), 0) // n
            == lax.broadcasted_iota(jnp.int32, (lanes, lanes), 1) // n)
    ones = jnp.where(same, 1.0, 0.0).astype(BF16)
    hi, lo = _split_bf16(x * x, 2)
    return (_dot(hi, ones) + _dot(lo, ones)) * (1.0 / n)


def _swa_kernel(qg_ref, kg_ref, sink_ref, bias_ref, q_ref, kp_ref, kc_ref, vp_ref, vc_ref, o_ref, *, blk):
    group = SW_HEADS // SW_KV_HEADS
    n_blk = q_ref.shape[0] // blk
    lo = lax.broadcasted_iota(jnp.int32, (1, LANES), 1) < SW_HD
    kk = jnp.concatenate([kp_ref[...], kc_ref[...]], axis=0)
    vv = jnp.concatenate([vp_ref[...], vc_ref[...]], axis=0)
    q_scale = qg_ref[...] * (SW_HD ** -0.5)
    kn, v_low, v_high, qst = {}, {}, {}, {}
    for kg in range(SW_KV_HEADS // 2):
        kx = kk[:, kg * LANES:(kg + 1) * LANES]
        vx = vv[:, kg * LANES:(kg + 1) * LANES]
        kx = kx * lax.rsqrt(_group_mean_sq(kx, SW_HD) + EPS) * kg_ref[...]
        kr = pltpu.roll(kx, SW_HD, axis=1)
        vr = pltpu.roll(vx, SW_HD, axis=1)
        for half in range(2):
            kv = 2 * kg + half
            kn[kv] = (jnp.where(lo, kx, kr) if half == 0 else jnp.where(lo, kr, kx)).astype(BF16)
            v_low[kv] = jnp.where(lo, vx if half == 0 else vr, 0.0).astype(BF16)
            v_high[kv] = jnp.where(lo, 0.0, vr if half == 0 else vx).astype(BF16)
    pairs = [(b, kv) for b in range(n_blk) for kv in range(SW_KV_HEADS)]
    for b, kv in pairs:
        qs = []
        for j in range(kv * group // 2, (kv + 1) * group // 2):
            qx = q_ref[b * blk:(b + 1) * blk, j * LANES:(j + 1) * LANES]
            qn = qx * lax.rsqrt(_group_mean_sq(qx, SW_HD) + EPS) * q_scale
            qs += [jnp.where(lo, qn, 0.0).astype(BF16), jnp.where(lo, 0.0, qn).astype(BF16)]
        qst[b, kv] = jnp.concatenate(qs, axis=0)
    s = {(b, kv): _dot_nt(kn[kv][b * blk:(b + 2) * blk], qst[b, kv]) + bias_ref[0 if b == 0 else 1, kv]
         for b, kv in pairs}
    pn = {}
    for b, kv in pairs:
        sink = sink_ref[:, kv * group * blk:(kv + 1) * group * blk]
        mx = jnp.maximum(jnp.max(s[b, kv], axis=0, keepdims=True), sink)
        p = jnp.exp(s[b, kv] - mx)
        inv = 1.0 / (jnp.sum(p, axis=0, keepdims=True) + jnp.exp(sink - mx))
        pn[b, kv] = (p * inv).astype(BF16)
    for b, kv in pairs:
        ks = slice(b * blk, (b + 2) * blk)
        for idx, j in enumerate(range(kv * group // 2, (kv + 1) * group // 2)):
            even = pn[b, kv][:, 2 * idx * blk:(2 * idx + 1) * blk]
            odd = pn[b, kv][:, (2 * idx + 1) * blk:(2 * idx + 2) * blk]
            o_ref[b * blk:(b + 1) * blk, j * LANES:(j + 1) * LANES] = (
                _dot_tn(even, v_low[kv][ks]) + _dot_tn(odd, v_high[kv][ks])).astype(o_ref.dtype)


def _swa_call(q, k, v, q_gain, k_gain, sinks):
    t, nq = q.shape
    nk = k.shape[1]
    b = SW_BLOCK
    n_blk = SW_TILE // b
    bias = _swa_bias(b)
    qg2 = jnp.tile(q_gain, (1, LANES // SW_HD))
    kg2 = jnp.tile(k_gain, (1, LANES // SW_HD))
    sink_row = jnp.repeat(sinks, b, axis=1)
    cur = lambda n: pl.BlockSpec((SW_TILE, n), lambda s: (s, 0))
    prev = lambda n: pl.BlockSpec((b, n), lambda s: (jnp.maximum(s * n_blk - 1, 0), 0))
    bias2 = jnp.stack([bias, jnp.stack([bias[1], bias[1]])])
    bias_spec = pl.BlockSpec((None,) + bias2.shape[1:], lambda s: (jnp.minimum(s, 1), 0, 0, 0, 0))
    return pl.pallas_call(
        functools.partial(_swa_kernel, blk=b), grid=(t // SW_TILE,),
        in_specs=[_resident(qg2.shape), _resident(kg2.shape), _resident(sink_row.shape), bias_spec,
                  cur(nq), prev(nk), cur(nk), prev(nk), cur(nk)],
        out_specs=cur(nq), out_shape=jax.ShapeDtypeStruct((t, nq), BF16),
        compiler_params=_params(("arbitrary",)), name="swa",
    )(qg2, kg2, sink_row, bias2, q, k, k, v, v)


def kernel(x, p, norm_gains, w_ffn_gu, w_ffn_down, w_ple_gate, w_ple_proj, hg_lower_bounds, hg_w_in,
           hg_g_norm, hg_w_out, ml_w_qkvo, ml_w_if, ml_b_if, ml_norm, ml_w_out, sw_w_qkv, sw_q_norm,
           sw_k_norm, sw_sinks, sw_w_o):
    batch, seq, d = x.shape
    depth = p.shape[0]
    w_in = (hg_w_in, ml_w_qkvo, sw_w_qkv)
    w_out = (hg_w_out, ml_w_out, sw_w_o)

    def pre_jobs(layer):
        return [(w_ffn_gu, (layer, 0)), (w_ffn_down, (layer, 0)), (w_in[layer % N_MIXERS], (layer // N_MIXERS,))]

    def post_jobs(layer):
        return [(w_out[layer % N_MIXERS], (layer // N_MIXERS,)), (w_ffn_gu, (layer, 1)), (w_ffn_down, (layer, 1)),
                (w_ple_gate, (layer,)), (w_ple_proj, (layer,))]

    outs = []
    for bi in range(batch):
        xs, xb = x, bi
        pre_w = [w[lead].astype(BF16) for w, lead in pre_jobs(0)]
        for layer in range(depth):
            kind, j = layer % N_MIXERS, layer // N_MIXERS
            pre = functools.partial(_pre_call, xs, xb, norm_gains, layer, pre_w, cast_jobs=post_jobs(layer))
            if kind == 0:
                h, q, f, i, og, *post_w = pre((d, d, d, d), (F32, F32, BF16, F32))
                mix = _hgrn2_call(q, f, i, og, hg_lower_bounds, hg_g_norm[j][None, :], layer)
            elif kind == 1:
                dqk = (ml_w_qkvo.shape[2] - 2 * d) // 2
                h, q, k, v, og, gates, gates_t, *post_w = pre((dqk, dqk, d, d), (BF16, F32, BF16, F32),
                                                              gate_w=ml_w_if[j], gate_b=ml_b_if[j])
                mix = _mlstm_call(q, k, v, og, gates, gates_t, ml_norm[j][None, :])
            else:
                nq = SW_HEADS * SW_HD
                nk = SW_KV_HEADS * SW_HD
                h, q, k, v, *post_w = pre((nq, nk, nk), (F32, F32, F32))
                mix = _swa_call(q, k, v, sw_q_norm[j][None, :], sw_k_norm[j][None, :], sw_sinks[j][None, :])
            jobs = pre_jobs(layer + 1) if layer + 1 < depth else []
            xs, *pre_w = _post_call(h, mix, p, bi, norm_gains, layer, post_w, cast_jobs=jobs)
            xb = 0
        outs.append(xs)
    return outs[0] if batch == 1 else jnp.concatenate(outs, axis=0)
```

```python
import functools

import numpy as np
import jax
import jax.numpy as jnp
from jax import lax
from jax.experimental import pallas as pl
from jax.experimental.pallas import tpu as pltpu

F32 = jnp.float32
BF16 = jnp.bfloat16

EPS = 1e-6
N_MIXERS = 3
HG_HEADS = 8
ML_HEADS = 4
ML_GATE_CAP = 15.0
SW_HEADS = 16
SW_KV_HEADS = 4
SW_HD = 64
SW_WINDOW = 128

V7X_VMEM_LIMIT_BYTES = 56 * 1024 * 1024
LANES = 128
BF16_SUBLANES = 16
F32_SUBLANES = 8
LOG2E = 1.4426950408889634

ROW_TILE = 512
FF_CHUNK = 256
HG_CHUNK = 128
HG_TILE = 256
ML_CHUNK = 256
SW_BLOCK = 128
SW_TILE = 256


def _dot(a, b):
    return jnp.dot(a, b, preferred_element_type=F32)


def _dot_nt(a, b):
    return lax.dot_general(a, b, (((1,), (1,)), ((), ())), preferred_element_type=F32)


def _dot_tn(a, b):
    return lax.dot_general(a, b, (((0,), (0,)), ((), ())), preferred_element_type=F32)


def _sigmoid(x):
    return 0.5 * jnp.tanh(0.5 * x) + 0.5


def _log_sigmoid(x):
    return jnp.minimum(x, 0.0) - jnp.log1p(jnp.exp(-jnp.abs(x)))


def _rms(x, gain):
    return x * lax.rsqrt(jnp.mean(x * x, axis=-1, keepdims=True) + EPS) * gain


def _split_bf16(x, terms):
    parts = []
    r = x
    for _ in range(terms):
        p = r.astype(BF16)
        parts.append(p)
        r = r - p.astype(F32)
    return parts


def _swiglu_residual(x, gain, wgu_ref, wd_ref):
    d_ff = wd_ref.shape[0]
    xn = _rms(x, gain).astype(BF16)
    y = jnp.zeros_like(x)
    for j in range(d_ff // FF_CHUNK):
        lo = j * FF_CHUNK
        g = _dot(xn, wgu_ref[:, lo:lo + FF_CHUNK])
        u = _dot(xn, wgu_ref[:, d_ff + lo:d_ff + lo + FF_CHUNK])
        a = (g * _sigmoid(g) * u).astype(BF16)
        y = y + _dot(a, wd_ref[lo:lo + FF_CHUNK, :])
    return x + 0.5 * y


def _resident(shape, lead=()):
    nd = len(shape)
    block = (None,) * len(lead) + tuple(shape[len(lead):])
    index = tuple(lead) + (0,) * (nd - len(lead))
    return pl.BlockSpec(block, lambda *_: index, pipeline_mode=pl.Buffered(1))


def _params(semantics):
    return pltpu.CompilerParams(dimension_semantics=semantics,
                                vmem_limit_bytes=V7X_VMEM_LIMIT_BYTES)


def _cast_specs(jobs, n_steps):
    in_specs, out_specs, out_shapes = [], [], []
    for w, lead in jobs:
        rows, cols = w.shape[len(lead):]
        k = next(k for k in (1, 2, 4, 8) if (rows * k) % (n_steps * BF16_SUBLANES) == 0)
        hb = rows * k // n_steps
        in_specs.append(pl.BlockSpec((None,) * len(lead) + (hb, cols),
                                     lambda i, lead=tuple(lead), k=k: lead + (i // k, 0)))
        out_specs.append(pl.BlockSpec((hb, cols), lambda i, k=k: (i // k, 0)))
        out_shapes.append(jax.ShapeDtypeStruct((rows, cols), BF16))
    return in_specs, out_specs, out_shapes


def _cast_slabs(in_refs, out_refs):
    for src, dst in zip(in_refs, out_refs):
        dst[...] = src[...].astype(dst.dtype)


def _pre_kernel(*refs, splits, n_gate, n_cast):
    n_in = 5 + (2 if n_gate else 0)
    ins, cast_in, outs = refs[:n_in], refs[n_in:n_in + n_cast], refs[n_in + n_cast:]
    x_ref, gains_ref, wgu_ref, wd_ref, win_ref = ins[:5]
    outs, cast_out = outs[:len(outs) - n_cast], outs[len(outs) - n_cast:]
    _cast_slabs(cast_in, cast_out)
    h_ref = outs[0]
    proj_refs = outs[1:1 + len(splits)]
    h = _swiglu_residual(x_ref[...], gains_ref[0:1, :], wgu_ref, wd_ref)
    h_ref[...] = h
    xn = _rms(h, gains_ref[1:2, :])
    xb = xn.astype(BF16)
    off = 0
    for r, n in zip(proj_refs, splits):
        r[...] = _dot(xb, win_ref[:, off:off + n]).astype(r.dtype)
        off += n
    if n_gate:
        wif_ref, bif_ref = ins[5:7]
        g_ref, gt_ref = outs[1 + len(splits):]
        xh, xl = _split_bf16(xn, 2)
        wh, wl = _split_bf16(wif_ref[...], 2)
        pre = _dot(xh, wh) + _dot(xl, wh) + _dot(xh, wl) + bif_ref[...]
        capped = ML_GATE_CAP * jnp.tanh(pre / ML_GATE_CAP)
        lane = lax.broadcasted_iota(jnp.int32, capped.shape, 1)
        gates = jnp.where(lane < n_gate // 2, capped, _log_sigmoid(capped))
        g_ref[...] = gates
        gt_ref[...] = gates.T[0:gt_ref.shape[0], :]


def _pre_call(x, bi, gains, layer, weights, splits, out_dtypes, gate_w=None, gate_b=None, cast_jobs=()):
    _, t, d = x.shape
    n_gate = 0 if gate_w is None else gate_w.shape[1]
    tm = ROW_TILE
    row = lambda n: pl.BlockSpec((tm, n), lambda i: (i, 0))
    in_specs = [pl.BlockSpec((None, tm, d), lambda i: (bi, i, 0)), _resident(gains.shape, (layer,))]
    in_specs += [_resident(w.shape) for w in weights]
    args = [x, gains, *weights]
    out_shape = [jax.ShapeDtypeStruct((t, d), F32)]
    out_specs = [row(d)]
    for n, dt in zip(splits, out_dtypes):
        out_shape.append(jax.ShapeDtypeStruct((t, n), dt))
        out_specs.append(row(n))
    if n_gate:
        wpad = jnp.zeros((d, LANES), F32).at[:, :n_gate].set(gate_w)
        bpad = jnp.zeros((1, LANES), F32).at[0, :n_gate].set(gate_b)
        in_specs += [_resident(wpad.shape), _resident(bpad.shape)]
        args += [wpad, bpad]
        out_shape += [jax.ShapeDtypeStruct((t, LANES), F32), jax.ShapeDtypeStruct((n_gate, t), F32)]
        out_specs += [row(LANES), pl.BlockSpec((n_gate, tm), lambda i: (0, i))]
    c_in, c_out, c_shape = _cast_specs(cast_jobs, t // tm)
    return pl.pallas_call(
        functools.partial(_pre_kernel, splits=tuple(splits), n_gate=n_gate, n_cast=len(cast_jobs)),
        grid=(t // tm,), in_specs=in_specs + c_in, out_specs=out_specs + c_out, out_shape=out_shape + c_shape,
        compiler_params=_params(("arbitrary",)), name="pre",
    )(*args, *[w for w, _ in cast_jobs])


def _post_kernel(*refs, n_cast):
    h_ref, o_ref, p_ref, gains_ref, wout_ref, wgu_ref, wd_ref, wpg_ref, wpp_ref = refs[:9]
    cast_in, x_ref, cast_out = refs[9:9 + n_cast], refs[9 + n_cast], refs[10 + n_cast:]
    _cast_slabs(cast_in, cast_out)
    h = h_ref[...] + _dot(o_ref[...], wout_ref[...])
    h = _swiglu_residual(h, gains_ref[2:3, :], wgu_ref, wd_ref)
    gate = _sigmoid(_dot(_rms(h, gains_ref[3:4, :]).astype(BF16), wpg_ref[...]))
    x_ref[...] = h + gate * _dot(p_ref[...].astype(BF16), wpp_ref[...])


def _post_call(h, o, p, bi, gains, layer, weights, cast_jobs=()):
    t, d = h.shape
    tm = ROW_TILE
    row = lambda n: pl.BlockSpec((tm, n), lambda i: (i, 0))
    c_in, c_out, c_shape = _cast_specs(cast_jobs, t // tm)
    return pl.pallas_call(
        functools.partial(_post_kernel, n_cast=len(cast_jobs)), grid=(t // tm,),
        in_specs=[row(d), row(o.shape[1]), pl.BlockSpec((None, None, tm, p.shape[3]), lambda i: (layer, bi, i, 0)),
                  _resident(gains.shape, (layer,))] + [_resident(w.shape) for w in weights] + c_in,
        out_specs=[pl.BlockSpec((None, tm, d), lambda i: (0, i, 0))] + c_out,
        out_shape=[jax.ShapeDtypeStruct((1, t, d), F32)] + c_shape,
        compiler_params=_params(("arbitrary",)), name="post",
    )(h, o, p, gains, *weights, *[w for w, _ in cast_jobs])


def _hgrn2_tables(c):
    n_lev = int(np.log2(c))
    assert 2 ** n_lev == c
    mats = np.zeros((n_lev + 2, c, c), np.float32)
    r = np.arange(c)[:, None]
    j = np.arange(c)[None, :]
    mats[0] = j <= r
    for lev in range(n_lev):
        half = 2 ** lev
        m = (r // (2 * half)) * (2 * half) + half
        mats[1 + lev] = np.where(r >= m, (j >= m) & (j <= r), (j > r) & (j <= m - 1))
    mats[n_lev + 1] = j > r
    x = r ^ j
    lvl = np.where(r > j, np.floor(np.log2(np.maximum(x, 1))).astype(np.int32), -1)
    lvl = np.where(r == j, n_lev, lvl).astype(np.int32)
    mats = mats.reshape(-1, c)
    return jnp.asarray(np.concatenate([mats, mats], axis=1), BF16), jnp.asarray(lvl), n_lev


def _hgrn2_kernel(lbraw_ref, gn_ref, mexp_ref, lvl_ref, q_ref, f_ref, i_ref, og_ref, o_ref, st_ref,
                  *, layer, n_lev, heads, c):
    rows, d = q_ref.shape
    dk = d // heads
    chunks = range(rows // c)
    hh = range(heads)
    rs = {ci: slice(ci * c, (ci + 1) * c) for ci in chunks}
    sl = {h: slice(h * dk, (h + 1) * dk) for h in hh}
    pairs = [(ci, h) for ci in chunks for h in hh]

    @pl.when(pl.program_id(0) == 0)
    def _():
        st_ref[...] = jnp.zeros_like(st_ref)

    lrows = [lbraw_ref[r:r + 1, :] for r in range(lbraw_ref.shape[0])]
    mx = functools.reduce(jnp.maximum, lrows)
    es = [jnp.exp(r - mx) for r in lrows]
    lb = sum(es[1:layer + 1], jnp.zeros_like(mx)) / sum(es)

    key, expo = {}, {}
    for ci in chunks:
        f = f_ref[rs[ci], :]
        t = jnp.exp(-jnp.abs(f))
        r = 1.0 / (1.0 + t)
        pos = f >= 0.0
        tiny = jnp.logical_and(lb <= 0.0, jnp.logical_not(pos))
        num = jnp.where(pos, 1.0 + lb * t, lb + t)
        logf = jnp.log(jnp.where(tiny, 1.0, num) * r) + jnp.where(tiny, f, 0.0)
        key[ci] = (1.0 - lb) * jnp.where(pos, t * r, r)
        expo[ci] = _dot(mexp_ref[...], jnp.concatenate(_split_bf16(logf * LOG2E, 2), axis=0))

    lvl = lvl_ref[...]
    row = lax.broadcasted_iota(jnp.int32, (c, 1), 0)
    qh, kh, qb, kb, vh, bh, att, o = {}, {}, {}, {}, {}, {}, {}, {}
    for ci, h in pairs:
        qv = q_ref[rs[ci], sl[h]]
        qh[ci, h] = qv * _sigmoid(qv)
        kh[ci, h] = key[ci][:, sl[h]]
        qb[ci, h], kb[ci, h] = qh[ci, h].astype(BF16), kh[ci, h].astype(BF16)
        vh[ci, h] = i_ref[rs[ci], sl[h]].astype(BF16)
        bh[ci, h] = expo[ci][0:c, sl[h]]
    for ci, h in pairs:
        att[ci, h] = jnp.where(lvl == n_lev, _dot_nt(qb[ci, h], kb[ci, h]), 0.0)
    for lev in range(n_lev):
        upper = (row & (1 << lev)) != 0
        for ci, h in pairs:
            e = jnp.exp2(expo[ci][(1 + lev) * c:(2 + lev) * c, sl[h]].astype(BF16))
            x = jnp.where(upper, qb[ci, h], kb[ci, h]) * e
            att[ci, h] = jnp.where(lvl == lev, _dot_nt(x, x), att[ci, h])
    st = {h: st_ref[h] for h in hh}
    for ci in chunks:
        for h in hh:
            o[ci, h] = (_dot_nt(qb[ci, h] * jnp.exp2(bh[ci, h].astype(BF16)), st[h].astype(BF16))
                        + _dot(att[ci, h].astype(BF16), vh[ci, h]))
        for h in hh:
            khat = kb[ci, h] * jnp.exp2(expo[ci][(n_lev + 1) * c:(n_lev + 2) * c, sl[h]].astype(BF16))
            st[h] = st[h] * jnp.exp2(bh[ci, h][c - 1:c, :]) + _dot_tn(vh[ci, h], khat)
    for h in hh:
        st_ref[h] = st[h]
    for ci, h in pairs:
        ogv = og_ref[rs[ci], sl[h]]
        o_ref[rs[ci], sl[h]] = (_rms(o[ci, h], gn_ref[...]) * (ogv * _sigmoid(ogv))).astype(o_ref.dtype)


def _hgrn2_call(q, f, i, og, lbraw, g_norm, layer):
    t, d = q.shape
    c = HG_CHUNK
    tile = HG_TILE
    mexp, lvl, n_lev = _hgrn2_tables(c)
    dk = d // HG_HEADS
    blk = pl.BlockSpec((tile, d), lambda n: (n, 0))
    return pl.pallas_call(
        functools.partial(_hgrn2_kernel, layer=layer, n_lev=n_lev, heads=HG_HEADS, c=c),
        grid=(t // tile,),
        in_specs=[_resident(lbraw.shape), _resident(g_norm.shape), _resident(mexp.shape),
                  _resident(lvl.shape), blk, blk, blk, blk],
        out_specs=blk, out_shape=jax.ShapeDtypeStruct((t, d), BF16),
        scratch_shapes=[pltpu.VMEM((HG_HEADS, dk, dk), F32)],
        compiler_params=_params(("arbitrary",)), name="hgrn2",
    )(lbraw, g_norm, mexp, lvl, q, f, i, og)


def _mlstm_kernel(tril_ref, norm_ref, q_ref, k_ref, v_ref, og_ref, g_ref, gt_ref, o_ref,
                  c_ref, n_ref, m_ref, *, heads):
    c = q_ref.shape[0]
    dqk = q_ref.shape[1] // heads
    dv = v_ref.shape[1] // heads

    @pl.when(pl.program_id(0) == 0)
    def _():
        c_ref[...] = jnp.zeros_like(c_ref)
        n_ref[...] = jnp.zeros_like(n_ref)
        m_ref[...] = jnp.zeros_like(m_ref)

    tril = tril_ref[...]
    gates = g_ref[...]
    gates_t = gt_ref[...]
    cum_col = sum(_dot(tril, part) for part in _split_bf16(gates, 3))
    cum_row = sum(_dot_nt(part, tril) for part in _split_bf16(gates_t, 3))
    causal = lax.broadcasted_iota(jnp.int32, (c, c), 0) >= lax.broadcasted_iota(jnp.int32, (c, c), 1)

    hh = range(heads)
    ones = jnp.ones((c, LANES), BF16)
    wide = lambda a, n: jnp.concatenate([a] * (n // LANES), axis=1)
    b_col = {h: cum_col[:, heads + h:heads + h + 1] for h in hh}
    b_row = {h: cum_row[heads + h:heads + h + 1, :] for h in hh}
    i_col = {h: gates[:, h:h + 1] for h in hh}
    i_row = {h: gates_t[h:h + 1, :] for h in hh}
    m_prev = {h: m_ref[h:h + 1, 0:1] for h in hh}
    qb = {h: q_ref[:, h * dqk:(h + 1) * dqk].astype(BF16) for h in hh}
    kh = {h: k_ref[:, h * dqk:(h + 1) * dqk] * (dqk ** -0.5) for h in hh}
    vh = {h: v_ref[:, h * dv:(h + 1) * dv].astype(BF16) for h in hh}
    cst = {h: c_ref[h] for h in hh}
    nst = {h: n_ref[h] for h in hh}

    b_rep = {h: jnp.broadcast_to(b_col[h], (c, LANES)) for h in hh}
    log_intra = {h: jnp.where(causal, wide(b_rep[h], c) - (b_row[h] - i_row[h]), -jnp.inf) for h in hh}
    m_col = {h: jnp.maximum(jnp.max(log_intra[h], axis=1, keepdims=True), b_col[h] + m_prev[h]) for h in hh}
    m_rep = {h: jnp.broadcast_to(m_col[h], (c, LANES)) for h in hh}
    w_intra = {h: jnp.exp(log_intra[h] - wide(m_rep[h], c)) for h in hh}
    w_inter = {h: jnp.exp(b_rep[h] + m_prev[h] - m_rep[h]) for h in hh}
    scores = {h: (_dot_nt(qb[h], kh[h].astype(BF16)) * w_intra[h]).astype(BF16) for h in hh}
    inter = {h: _dot(qb[h], cst[h].astype(BF16)) for h in hh}
    num = {h: wide(w_inter[h], dv) * inter[h] + _dot(scores[h], vh[h]) for h in hh}
    den = {h: w_inter[h] * _dot(qb[h], nst[h].astype(BF16)) + _dot(scores[h], ones) for h in hh}
    inv = {h: 1.0 / jnp.maximum(jnp.abs(den[h]), jnp.exp(-m_rep[h])) for h in hh}
    hid = {h: num[h] * wide(inv[h], dv) for h in hh}

    b_last = {h: b_col[h][c - 1:c, :] for h in hh}
    log_state = {h: b_last[h] + m_prev[h] for h in hh}
    m_new = {h: jnp.maximum(log_state[h], jnp.max(b_last[h] - b_row[h] + i_row[h], axis=1, keepdims=True))
             for h in hh}
    w_src = {h: jnp.broadcast_to(jnp.exp(b_last[h] - b_col[h] + i_col[h] - m_new[h]), (c, LANES)) for h in hh}
    decay = {h: jnp.exp(log_state[h] - m_new[h]) for h in hh}
    khat = {h: (kh[h] * w_src[h]).astype(BF16) for h in hh}
    for h in hh:
        c_ref[h] = decay[h] * cst[h] + _dot_tn(khat[h], vh[h])
        n_ref[h] = decay[h] * nst[h] + _dot_tn(khat[h], ones)
        m_ref[h:h + 1, :] = jnp.broadcast_to(m_new[h], (1, m_ref.shape[1]))
    ones_dv = jnp.ones((dv, LANES), BF16)
    for h in hh:
        sq_hi, sq_lo = _split_bf16(hid[h] * hid[h], 2)
        ms = (_dot(sq_hi, ones_dv) + _dot(sq_lo, ones_dv)) * (1.0 / dv)
        ogv = og_ref[:, h * dv:(h + 1) * dv]
        o_ref[:, h * dv:(h + 1) * dv] = (hid[h] * wide(lax.rsqrt(ms + EPS), dv) * norm_ref[:, h * dv:(h + 1) * dv]
                                         * _sigmoid(ogv)).astype(o_ref.dtype)


def _mlstm_call(q, k, v, og, gates, gates_t, norm):
    t = q.shape[0]
    d = v.shape[1]
    c = ML_CHUNK
    dqk = q.shape[1] // ML_HEADS
    dv = d // ML_HEADS
    tril = jnp.asarray(np.tril(np.ones((c, c), np.float32)), BF16)
    blk = lambda n: pl.BlockSpec((c, n), lambda s: (s, 0))
    return pl.pallas_call(
        functools.partial(_mlstm_kernel, heads=ML_HEADS),
        grid=(t // c,),
        in_specs=[_resident(tril.shape), _resident(norm.shape), blk(q.shape[1]), blk(k.shape[1]),
                  blk(d), blk(d), blk(LANES), pl.BlockSpec((2 * ML_HEADS, c), lambda s: (0, s))],
        out_specs=blk(d), out_shape=jax.ShapeDtypeStruct((t, d), BF16),
        scratch_shapes=[pltpu.VMEM((ML_HEADS, dqk, dv), F32), pltpu.VMEM((ML_HEADS, dqk, LANES), F32),
                        pltpu.VMEM((F32_SUBLANES, LANES), F32)],
        compiler_params=_params(("arbitrary",)), name="mlstm",
    )(tril, norm, q, k, v, og, gates, gates_t)


def _swa_bias(blk):
    group = SW_HEADS // SW_KV_HEADS
    kpos = np.arange(2 * blk)[:, None] - blk
    dist = np.arange(blk)[None, :] - kpos
    slopes = 2.0 ** (-8.0 * (np.arange(SW_HEADS) + 1) / SW_HEADS)
    ok = (dist >= 0) & (dist < SW_WINDOW)
    tables = []
    for first in (True, False):
        valid = ok & ((kpos >= 0) | (not first))
        bias = np.where(valid[None], -slopes[:, None, None] * dist[None] * LOG2E, -np.inf)
        bias = bias.reshape(SW_KV_HEADS, group, 2 * blk, blk).transpose(0, 2, 1, 3)
        tables.append(bias.reshape(SW_KV_HEADS, 2 * blk, group * blk))
    return jnp.asarray(np.stack(tables), F32)


def _group_mean_sq(x, n):
    lanes = x.shape[1]
    same = (lax.broadcasted_iota(jnp.int32, (lanes, lanes), 0) // n
            == lax.broadcasted_iota(jnp.int32, (lanes, lanes), 1) // n)
    ones = jnp.where(same, 1.0, 0.0).astype(BF16)
    hi, lo = _split_bf16(x * x, 2)
    return (_dot(hi, ones) + _dot(lo, ones)) * (1.0 / n)


def _swa_kernel(qg_ref, kg_ref, sink_ref, bias_ref, q_ref, kp_ref, kc_ref, vp_ref, vc_ref, o_ref, *, blk):
    group = SW_HEADS // SW_KV_HEADS
    n_blk = q_ref.shape[0] // blk
    lo = lax.broadcasted_iota(jnp.int32, (1, LANES), 1) < SW_HD
    kk = jnp.concatenate([kp_ref[...], kc_ref[...]], axis=0)
    vv = jnp.concatenate([vp_ref[...], vc_ref[...]], axis=0)
    q_scale = qg_ref[...] * (SW_HD ** -0.5 * LOG2E)
    kn, v_low, v_high, qst = {}, {}, {}, {}
    for kg in range(SW_KV_HEADS // 2):
        kx = kk[:, kg * LANES:(kg + 1) * LANES]
        vx = vv[:, kg * LANES:(kg + 1) * LANES]
        kx = kx * lax.rsqrt(_group_mean_sq(kx, SW_HD) + EPS) * kg_ref[...]
        kr = pltpu.roll(kx, SW_HD, axis=1)
        vr = pltpu.roll(vx, SW_HD, axis=1)
        for half in range(2):
            kv = 2 * kg + half
            kn[kv] = (jnp.where(lo, kx, kr) if half == 0 else jnp.where(lo, kr, kx)).astype(BF16)
            v_low[kv] = jnp.where(lo, vx if half == 0 else vr, 0.0).astype(BF16)
            v_high[kv] = jnp.where(lo, 0.0, vr if half == 0 else vx).astype(BF16)
    pairs = [(b, kv) for b in range(n_blk) for kv in range(SW_KV_HEADS)]
    for b, kv in pairs:
        qs = []
        for j in range(kv * group // 2, (kv + 1) * group // 2):
            qx = q_ref[b * blk:(b + 1) * blk, j * LANES:(j + 1) * LANES]
            qn = qx * lax.rsqrt(_group_mean_sq(qx, SW_HD) + EPS) * q_scale
            qs += [jnp.where(lo, qn, 0.0).astype(BF16), jnp.where(lo, 0.0, qn).astype(BF16)]
        qst[b, kv] = jnp.concatenate(qs, axis=0)
    s = {(b, kv): _dot_nt(kn[kv][b * blk:(b + 2) * blk], qst[b, kv]) + bias_ref[0 if b == 0 else 1, kv]
         for b, kv in pairs}
    pn = {}
    for b, kv in pairs:
        sink = sink_ref[:, kv * group * blk:(kv + 1) * group * blk]
        mx = jnp.maximum(jnp.max(s[b, kv], axis=0, keepdims=True), sink)
        p = jnp.exp2(s[b, kv] - mx)
        inv = 1.0 / (jnp.sum(p, axis=0, keepdims=True) + jnp.exp2(sink - mx))
        pn[b, kv] = (p * inv).astype(BF16)
    for b, kv in pairs:
        ks = slice(b * blk, (b + 2) * blk)
        for idx, j in enumerate(range(kv * group // 2, (kv + 1) * group // 2)):
            even = pn[b, kv][:, 2 * idx * blk:(2 * idx + 1) * blk]
            odd = pn[b, kv][:, (2 * idx + 1) * blk:(2 * idx + 2) * blk]
            o_ref[b * blk:(b + 1) * blk, j * LANES:(j + 1) * LANES] = (
                _dot_tn(even, v_low[kv][ks]) + _dot_tn(odd, v_high[kv][ks])).astype(o_ref.dtype)


def _swa_call(q, k, v, q_gain, k_gain, sinks):
    t, nq = q.shape
    nk = k.shape[1]
    b = SW_BLOCK
    n_blk = SW_TILE // b
    bias = _swa_bias(b)
    qg2 = jnp.tile(q_gain, (1, LANES // SW_HD))
    kg2 = jnp.tile(k_gain, (1, LANES // SW_HD))
    sink_row = jnp.repeat(sinks, b, axis=1) * LOG2E
    cur = lambda n: pl.BlockSpec((SW_TILE, n), lambda s: (s, 0))
    prev = lambda n: pl.BlockSpec((b, n), lambda s: (jnp.maximum(s * n_blk - 1, 0), 0))
    bias2 = jnp.stack([bias, jnp.stack([bias[1], bias[1]])])
    bias_spec = pl.BlockSpec((None,) + bias2.shape[1:], lambda s: (jnp.minimum(s, 1), 0, 0, 0, 0))
    return pl.pallas_call(
        functools.partial(_swa_kernel, blk=b), grid=(t // SW_TILE,),
        in_specs=[_resident(qg2.shape), _resident(kg2.shape), _resident(sink_row.shape), bias_spec,
                  cur(nq), prev(nk), cur(nk), prev(nk), cur(nk)],
        out_specs=cur(nq), out_shape=jax.ShapeDtypeStruct((t, nq), BF16),
        compiler_params=_params(("arbitrary",)), name="swa",
    )(qg2, kg2, sink_row, bias2, q, k, k, v, v)


def kernel(x, p, norm_gains, w_ffn_gu, w_ffn_down, w_ple_gate, w_ple_proj, hg_lower_bounds, hg_w_in,
           hg_g_norm, hg_w_out, ml_w_qkvo, ml_w_if, ml_b_if, ml_norm, ml_w_out, sw_w_qkv, sw_q_norm,
           sw_k_norm, sw_sinks, sw_w_o):
    batch, seq, d = x.shape
    depth = p.shape[0]
    w_in = (hg_w_in, ml_w_qkvo, sw_w_qkv)
    w_out = (hg_w_out, ml_w_out, sw_w_o)

    def pre_jobs(layer):
        return [(w_ffn_gu, (layer, 0)), (w_ffn_down, (layer, 0)), (w_in[layer % N_MIXERS], (layer // N_MIXERS,))]

    def post_jobs(layer):
        return [(w_out[layer % N_MIXERS], (layer // N_MIXERS,)), (w_ffn_gu, (layer, 1)), (w_ffn_down, (layer, 1)),
                (w_ple_gate, (layer,)), (w_ple_proj, (layer,))]

    outs = []
    for bi in range(batch):
        xs, xb = x, bi
        pre_w = [w[lead].astype(BF16) for w, lead in pre_jobs(0)]
        for layer in range(depth):
            kind, j = layer % N_MIXERS, layer // N_MIXERS
            pre = functools.partial(_pre_call, xs, xb, norm_gains, layer, pre_w, cast_jobs=post_jobs(layer))
            if kind == 0:
                h, q, f, i, og, *post_w = pre((d, d, d, d), (F32, F32, F32, F32))
                mix = _hgrn2_call(q, f, i, og, hg_lower_bounds, hg_g_norm[j][None, :], layer)
            elif kind == 1:
                dqk = (ml_w_qkvo.shape[2] - 2 * d) // 2
                h, q, k, v, og, gates, gates_t, *post_w = pre((dqk, dqk, d, d), (F32, F32, F32, F32),
                                                              gate_w=ml_w_if[j], gate_b=ml_b_if[j])
                mix = _mlstm_call(q, k, v, og, gates, gates_t, ml_norm[j][None, :])
            else:
                nq = SW_HEADS * SW_HD
                nk = SW_KV_HEADS * SW_HD
                h, q, k, v, *post_w = pre((nq, nk, nk), (F32, F32, F32))
                mix = _swa_call(q, k, v, sw_q_norm[j][None, :], sw_k_norm[j][None, :], sw_sinks[j][None, :])
            jobs = pre_jobs(layer + 1) if layer + 1 < depth else []
            xs, *pre_w = _post_call(h, mix, p, bi, norm_gains, layer, post_w, cast_jobs=jobs)
            xb = 0
        outs.append(xs)
    return outs[0] if batch == 1 else jnp.concatenate(outs, axis=0)
```

```python
import functools

import numpy as np
import jax
import jax.numpy as jnp
from jax import lax
from jax.experimental import pallas as pl
from jax.experimental.pallas import tpu as pltpu

F32 = jnp.float32
BF16 = jnp.bfloat16

EPS = 1e-6
N_MIXERS = 3
HG_HEADS = 8
ML_HEADS = 4
ML_GATE_CAP = 15.0
SW_HEADS = 16
SW_KV_HEADS = 4
SW_HD = 64
SW_WINDOW = 128

V7X_VMEM_LIMIT_BYTES = 56 * 1024 * 1024
LANES = 128
BF16_SUBLANES = 16
F32_SUBLANES = 8
LOG2E = 1.4426950408889634

ROW_TILE = 512
FF_CHUNK = 256
HG_CHUNK = 128
HG_TILE = 256
ML_CHUNK = 256
SW_BLOCK = 128
SW_TILE = 256


def _dot(a, b):
    return jnp.dot(a, b, preferred_element_type=F32)


def _dot_nt(a, b):
    return lax.dot_general(a, b, (((1,), (1,)), ((), ())), preferred_element_type=F32)


def _dot_tn(a, b):
    return lax.dot_general(a, b, (((0,), (0,)), ((), ())), preferred_element_type=F32)


def _sigmoid(x):
    return 0.5 * jnp.tanh(0.5 * x) + 0.5


def _log_sigmoid(x):
    return jnp.minimum(x, 0.0) - jnp.log1p(jnp.exp(-jnp.abs(x)))


def _rms(x, gain):
    return x * lax.rsqrt(jnp.mean(x * x, axis=-1, keepdims=True) + EPS) * gain


def _split_bf16(x, terms):
    parts = []
    r = x
    for _ in range(terms):
        p = r.astype(BF16)
        parts.append(p)
        r = r - p.astype(F32)
    return parts


def _swiglu_residual(x, gain, wgu_ref, wd_ref):
    d_ff = wd_ref.shape[0]
    xn = _rms(x, gain).astype(BF16)
    y = jnp.zeros_like(x)
    for j in range(d_ff // FF_CHUNK):
        lo = j * FF_CHUNK
        g = _dot(xn, wgu_ref[:, lo:lo + FF_CHUNK])
        u = _dot(xn, wgu_ref[:, d_ff + lo:d_ff + lo + FF_CHUNK])
        a = (g * _sigmoid(g) * u).astype(BF16)
        y = y + _dot(a, wd_ref[lo:lo + FF_CHUNK, :])
    return x + 0.5 * y


def _resident(shape, lead=()):
    nd = len(shape)
    block = (None,) * len(lead) + tuple(shape[len(lead):])
    index = tuple(lead) + (0,) * (nd - len(lead))
    return pl.BlockSpec(block, lambda *_: index, pipeline_mode=pl.Buffered(1))


def _params(semantics):
    return pltpu.CompilerParams(dimension_semantics=semantics,
                                vmem_limit_bytes=V7X_VMEM_LIMIT_BYTES)


def _cast_specs(jobs, n_steps):
    in_specs, out_specs, out_shapes = [], [], []
    for w, lead in jobs:
        rows, cols = w.shape[len(lead):]
        k = next(k for k in (1, 2, 4, 8) if (rows * k) % (n_steps * BF16_SUBLANES) == 0)
        hb = rows * k // n_steps
        in_specs.append(pl.BlockSpec((None,) * len(lead) + (hb, cols),
                                     lambda i, lead=tuple(lead), k=k: lead + (i // k, 0)))
        out_specs.append(pl.BlockSpec((hb, cols), lambda i, k=k: (i // k, 0)))
        out_shapes.append(jax.ShapeDtypeStruct((rows, cols), BF16))
    return in_specs, out_specs, out_shapes


def _cast_slabs(in_refs, out_refs):
    for src, dst in zip(in_refs, out_refs):
        dst[...] = src[...].astype(dst.dtype)


def _pre_kernel(*refs, splits, n_gate, n_cast):
    n_in = 5 + (2 if n_gate else 0)
    ins, cast_in, outs = refs[:n_in], refs[n_in:n_in + n_cast], refs[n_in + n_cast:]
    x_ref, gains_ref, wgu_ref, wd_ref, win_ref = ins[:5]
    outs, cast_out = outs[:len(outs) - n_cast], outs[len(outs) - n_cast:]
    _cast_slabs(cast_in, cast_out)
    h_ref = outs[0]
    proj_refs = outs[1:1 + len(splits)]
    h = _swiglu_residual(x_ref[...], gains_ref[0:1, :], wgu_ref, wd_ref)
    h_ref[...] = h
    xn = _rms(h, gains_ref[1:2, :])
    xb = xn.astype(BF16)
    off = 0
    for r, n in zip(proj_refs, splits):
        r[...] = _dot(xb, win_ref[:, off:off + n]).astype(r.dtype)
        off += n
    if n_gate:
        wif_ref, bif_ref = ins[5:7]
        g_ref, gt_ref = outs[1 + len(splits):]
        xh, xl = _split_bf16(xn, 2)
        wh, wl = _split_bf16(wif_ref[...], 2)
        pre = _dot(xh, wh) + _dot(xl, wh) + _dot(xh, wl) + bif_ref[...]
        capped = ML_GATE_CAP * jnp.tanh(pre / ML_GATE_CAP)
        lane = lax.broadcasted_iota(jnp.int32, capped.shape, 1)
        gates = jnp.where(lane < n_gate // 2, capped, _log_sigmoid(capped))
        g_ref[...] = gates
        gt_ref[...] = gates.T[0:gt_ref.shape[0], :]


def _pre_call(x, bi, gains, layer, weights, splits, out_dtypes, gate_w=None, gate_b=None, cast_jobs=()):
    _, t, d = x.shape
    n_gate = 0 if gate_w is None else gate_w.shape[1]
    tm = ROW_TILE
    row = lambda n: pl.BlockSpec((tm, n), lambda i: (i, 0))
    in_specs = [pl.BlockSpec((None, tm, d), lambda i: (bi, i, 0)), _resident(gains.shape, (layer,))]
    in_specs += [_resident(w.shape) for w in weights]
    args = [x, gains, *weights]
    out_shape = [jax.ShapeDtypeStruct((t, d), F32)]
    out_specs = [row(d)]
    for n, dt in zip(splits, out_dtypes):
        out_shape.append(jax.ShapeDtypeStruct((t, n), dt))
        out_specs.append(row(n))
    if n_gate:
        wpad = jnp.zeros((d, LANES), F32).at[:, :n_gate].set(gate_w)
        bpad = jnp.zeros((1, LANES), F32).at[0, :n_gate].set(gate_b)
        in_specs += [_resident(wpad.shape), _resident(bpad.shape)]
        args += [wpad, bpad]
        out_shape += [jax.ShapeDtypeStruct((t, LANES), F32), jax.ShapeDtypeStruct((n_gate, t), F32)]
        out_specs += [row(LANES), pl.BlockSpec((n_gate, tm), lambda i: (0, i))]
    c_in, c_out, c_shape = _cast_specs(cast_jobs, t // tm)
    return pl.pallas_call(
        functools.partial(_pre_kernel, splits=tuple(splits), n_gate=n_gate, n_cast=len(cast_jobs)),
        grid=(t // tm,), in_specs=in_specs + c_in, out_specs=out_specs + c_out, out_shape=out_shape + c_shape,
        compiler_params=_params(("arbitrary",)), name="pre",
    )(*args, *[w for w, _ in cast_jobs])


def _post_kernel(*refs, n_cast):
    h_ref, o_ref, p_ref, gains_ref, wout_ref, wgu_ref, wd_ref, wpg_ref, wpp_ref = refs[:9]
    cast_in, x_ref, cast_out = refs[9:9 + n_cast], refs[9 + n_cast], refs[10 + n_cast:]
    _cast_slabs(cast_in, cast_out)
    h = h_ref[...] + _dot(o_ref[...], wout_ref[...])
    h = _swiglu_residual(h, gains_ref[2:3, :], wgu_ref, wd_ref)
    gate = _sigmoid(_dot(_rms(h, gains_ref[3:4, :]).astype(BF16), wpg_ref[...]))
    x_ref[...] = h + gate * _dot(p_ref[...].astype(BF16), wpp_ref[...])


def _post_call(h, o, p, bi, gains, layer, weights, cast_jobs=()):
    t, d = h.shape
    tm = ROW_TILE
    row = lambda n: pl.BlockSpec((tm, n), lambda i: (i, 0))
    c_in, c_out, c_shape = _cast_specs(cast_jobs, t // tm)
    return pl.pallas_call(
        functools.partial(_post_kernel, n_cast=len(cast_jobs)), grid=(t // tm,),
        in_specs=[row(d), row(o.shape[1]), pl.BlockSpec((None, None, tm, p.shape[3]), lambda i: (layer, bi, i, 0)),
                  _resident(gains.shape, (layer,))] + [_resident(w.shape) for w in weights] + c_in,
        out_specs=[pl.BlockSpec((None, tm, d), lambda i: (0, i, 0))] + c_out,
        out_shape=[jax.ShapeDtypeStruct((1, t, d), F32)] + c_shape,
        compiler_params=_params(("arbitrary",)), name="post",
    )(h, o, p, gains, *weights, *[w for w, _ in cast_jobs])


def _hgrn2_tables(c):
    n_lev = int(np.log2(c))
    assert 2 ** n_lev == c
    mats = np.zeros((n_lev, c, c), np.float32)
    r = np.arange(c)[:, None]
    j = np.arange(c)[None, :]
    mats[0] = j <= r
    for lev in range(n_lev - 1):
        half = 2 ** lev
        m = (r // (2 * half)) * (2 * half) + half
        mats[1 + lev] = np.where(r >= m, (j >= m) & (j <= r), (j > r) & (j <= m - 1))
    x = r ^ j
    lvl = np.where(r > j, np.floor(np.log2(np.maximum(x, 1))).astype(np.int32), -1)
    lvl = np.where(r == j, n_lev, lvl).astype(np.int32)
    mats = mats.reshape(-1, c)
    return jnp.asarray(np.concatenate([mats, mats], axis=1), BF16), jnp.asarray(lvl), n_lev


def _hgrn2_kernel(lbraw_ref, gn_ref, mexp_ref, lvl_ref, q_ref, f_ref, i_ref, og_ref, o_ref, st_ref,
                  *, layer, n_lev, heads, c):
    rows, d = q_ref.shape
    dk = d // heads
    chunks = range(rows // c)
    hh = range(heads)
    rs = {ci: slice(ci * c, (ci + 1) * c) for ci in chunks}
    sl = {h: slice(h * dk, (h + 1) * dk) for h in hh}
    pairs = [(ci, h) for ci in chunks for h in hh]

    @pl.when(pl.program_id(0) == 0)
    def _():
        st_ref[...] = jnp.zeros_like(st_ref)

    lrows = [lbraw_ref[r:r + 1, :] for r in range(lbraw_ref.shape[0])]
    mx = functools.reduce(jnp.maximum, lrows)
    es = [jnp.exp(r - mx) for r in lrows]
    lb = sum(es[1:layer + 1], jnp.zeros_like(mx)) / sum(es)

    key, expo = {}, {}
    for ci in chunks:
        f = f_ref[rs[ci], :]
        t = jnp.exp(-jnp.abs(f))
        r = 1.0 / (1.0 + t)
        pos = f >= 0.0
        tiny = jnp.logical_and(lb <= 0.0, jnp.logical_not(pos))
        num = jnp.where(pos, 1.0 + lb * t, lb + t)
        logf = jnp.log(jnp.where(tiny, 1.0, num) * r) + jnp.where(tiny, f, 0.0)
        key[ci] = (1.0 - lb) * jnp.where(pos, t * r, r)
        expo[ci] = _dot(mexp_ref[...], jnp.concatenate(_split_bf16(logf * LOG2E, 2), axis=0))

    lvl = lvl_ref[...]
    row = lax.broadcasted_iota(jnp.int32, (c, 1), 0)
    qh, kh, qb, kb, vh, bh, att, o = {}, {}, {}, {}, {}, {}, {}, {}
    for ci, h in pairs:
        qv = q_ref[rs[ci], sl[h]]
        qh[ci, h] = qv * _sigmoid(qv)
        kh[ci, h] = key[ci][:, sl[h]]
        qb[ci, h], kb[ci, h] = qh[ci, h].astype(BF16), kh[ci, h].astype(BF16)
        vh[ci, h] = i_ref[rs[ci], sl[h]].astype(BF16)
        bh[ci, h] = expo[ci][0:c, sl[h]]
    for ci, h in pairs:
        att[ci, h] = jnp.where(lvl == n_lev, _dot_nt(qb[ci, h], kb[ci, h]), 0.0)
    for lev in range(n_lev):
        upper = (row & (1 << lev)) != 0
        for ci, h in pairs:
            if lev == n_lev - 1:
                ex = -jnp.abs(bh[ci, h] - bh[ci, h][c // 2 - 1:c // 2, :])
            else:
                ex = expo[ci][(1 + lev) * c:(2 + lev) * c, sl[h]]
            e = jnp.exp2(ex.astype(BF16))
            x = jnp.where(upper, qb[ci, h], kb[ci, h]) * e
            att[ci, h] = jnp.where(lvl == lev, _dot_nt(x, x), att[ci, h])
    st = {h: st_ref[h] for h in hh}
    for ci in chunks:
        for h in hh:
            o[ci, h] = (_dot_nt(qb[ci, h] * jnp.exp2(bh[ci, h].astype(BF16)), st[h].astype(BF16))
                        + _dot(att[ci, h].astype(BF16), vh[ci, h]))
        for h in hh:
            khat = kb[ci, h] * jnp.exp2((bh[ci, h][c - 1:c, :] - bh[ci, h]).astype(BF16))
            st[h] = st[h] * jnp.exp2(bh[ci, h][c - 1:c, :]) + _dot_tn(vh[ci, h], khat)
    for h in hh:
        st_ref[h] = st[h]
    for ci, h in pairs:
        ogv = og_ref[rs[ci], sl[h]]
        o_ref[rs[ci], sl[h]] = (_rms(o[ci, h], gn_ref[...]) * (ogv * _sigmoid(ogv))).astype(o_ref.dtype)


def _hgrn2_call(q, f, i, og, lbraw, g_norm, layer):
    t, d = q.shape
    c = HG_CHUNK
    tile = HG_TILE
    mexp, lvl, n_lev = _hgrn2_tables(c)
    dk = d // HG_HEADS
    blk = pl.BlockSpec((tile, d), lambda n: (n, 0))
    return pl.pallas_call(
        functools.partial(_hgrn2_kernel, layer=layer, n_lev=n_lev, heads=HG_HEADS, c=c),
        grid=(t // tile,),
        in_specs=[_resident(lbraw.shape), _resident(g_norm.shape), _resident(mexp.shape),
                  _resident(lvl.shape), blk, blk, blk, blk],
        out_specs=blk, out_shape=jax.ShapeDtypeStruct((t, d), BF16),
        scratch_shapes=[pltpu.VMEM((HG_HEADS, dk, dk), F32)],
        compiler_params=_params(("arbitrary",)), name="hgrn2",
    )(lbraw, g_norm, mexp, lvl, q, f, i, og)


def _mlstm_kernel(tril_ref, norm_ref, q_ref, k_ref, v_ref, og_ref, g_ref, gt_ref, o_ref,
                  c_ref, n_ref, m_ref, *, heads):
    c = q_ref.shape[0]
    dqk = q_ref.shape[1] // heads
    dv = v_ref.shape[1] // heads

    @pl.when(pl.program_id(0) == 0)
    def _():
        c_ref[...] = jnp.zeros_like(c_ref)
        n_ref[...] = jnp.zeros_like(n_ref)
        m_ref[...] = jnp.zeros_like(m_ref)

    tril = tril_ref[...]
    gates = g_ref[...]
    gates_t = gt_ref[...]
    cum_col = sum(_dot(tril, part) for part in _split_bf16(gates, 3))
    cum_row = sum(_dot_nt(part, tril) for part in _split_bf16(gates_t, 3))
    causal = lax.broadcasted_iota(jnp.int32, (c, c), 0) >= lax.broadcasted_iota(jnp.int32, (c, c), 1)

    hh = range(heads)
    ones = jnp.ones((c, LANES), BF16)
    wide = lambda a, n: jnp.concatenate([a] * (n // LANES), axis=1)
    b_col = {h: cum_col[:, heads + h:heads + h + 1] for h in hh}
    b_row = {h: cum_row[heads + h:heads + h + 1, :] for h in hh}
    i_col = {h: gates[:, h:h + 1] for h in hh}
    i_row = {h: gates_t[h:h + 1, :] for h in hh}
    m_prev = {h: m_ref[h:h + 1, 0:1] for h in hh}
    qb = {h: q_ref[:, h * dqk:(h + 1) * dqk].astype(BF16) for h in hh}
    kh = {h: k_ref[:, h * dqk:(h + 1) * dqk] * (dqk ** -0.5) for h in hh}
    vh = {h: v_ref[:, h * dv:(h + 1) * dv].astype(BF16) for h in hh}
    cst = {h: c_ref[h] for h in hh}
    nst = {h: n_ref[h] for h in hh}

    b_rep = {h: jnp.broadcast_to(b_col[h], (c, LANES)) for h in hh}
    log_intra = {h: jnp.where(causal, wide(b_rep[h], c) - (b_row[h] - i_row[h]), -jnp.inf) for h in hh}
    m_col = {h: jnp.maximum(jnp.max(log_intra[h], axis=1, keepdims=True), b_col[h] + m_prev[h]) for h in hh}
    m_rep = {h: jnp.broadcast_to(m_col[h], (c, LANES)) for h in hh}
    w_intra = {h: jnp.exp(log_intra[h] - wide(m_rep[h], c)) for h in hh}
    w_inter = {h: jnp.exp(b_rep[h] + m_prev[h] - m_rep[h]) for h in hh}
    scores = {h: (_dot_nt(qb[h], kh[h].astype(BF16)) * w_intra[h]).astype(BF16) for h in hh}
    inter = {h: _dot(qb[h], cst[h].astype(BF16)) for h in hh}
    num = {h: wide(w_inter[h], dv) * inter[h] + _dot(scores[h], vh[h]) for h in hh}
    den = {h: w_inter[h] * _dot(qb[h], nst[h].astype(BF16)) + _dot(scores[h], ones) for h in hh}
    inv = {h: 1.0 / jnp.maximum(jnp.abs(den[h]), jnp.exp(-m_rep[h])) for h in hh}
    hid = {h: num[h] * wide(inv[h], dv) for h in hh}

    b_last = {h: b_col[h][c - 1:c, :] for h in hh}
    log_state = {h: b_last[h] + m_prev[h] for h in hh}
    m_new = {h: jnp.maximum(log_state[h], jnp.max(b_last[h] - b_row[h] + i_row[h], axis=1, keepdims=True))
             for h in hh}
    w_src = {h: jnp.broadcast_to(jnp.exp(b_last[h] - b_col[h] + i_col[h] - m_new[h]), (c, LANES)) for h in hh}
    decay = {h: jnp.exp(log_state[h] - m_new[h]) for h in hh}
    khat = {h: (kh[h] * w_src[h]).astype(BF16) for h in hh}
    for h in hh:
        c_ref[h] = decay[h] * cst[h] + _dot_tn(khat[h], vh[h])
        n_ref[h] = decay[h] * nst[h] + _dot_tn(khat[h], ones)
        m_ref[h:h + 1, :] = jnp.broadcast_to(m_new[h], (1, m_ref.shape[1]))
    ones_dv = jnp.ones((dv, LANES), BF16)
    for h in hh:
        sq_hi, sq_lo = _split_bf16(hid[h] * hid[h], 2)
        ms = (_dot(sq_hi, ones_dv) + _dot(sq_lo, ones_dv)) * (1.0 / dv)
        ogv = og_ref[:, h * dv:(h + 1) * dv]
        o_ref[:, h * dv:(h + 1) * dv] = (hid[h] * wide(lax.rsqrt(ms + EPS), dv) * norm_ref[:, h * dv:(h + 1) * dv]
                                         * _sigmoid(ogv)).astype(o_ref.dtype)


def _mlstm_call(q, k, v, og, gates, gates_t, norm):
    t = q.shape[0]
    d = v.shape[1]
    c = ML_CHUNK
    dqk = q.shape[1] // ML_HEADS
    dv = d // ML_HEADS
    tril = jnp.asarray(np.tril(np.ones((c, c), np.float32)), BF16)
    blk = lambda n: pl.BlockSpec((c, n), lambda s: (s, 0))
    return pl.pallas_call(
        functools.partial(_mlstm_kernel, heads=ML_HEADS),
        grid=(t // c,),
        in_specs=[_resident(tril.shape), _resident(norm.shape), blk(q.shape[1]), blk(k.shape[1]),
                  blk(d), blk(d), blk(LANES), pl.BlockSpec((2 * ML_HEADS, c), lambda s: (0, s))],
        out_specs=blk(d), out_shape=jax.ShapeDtypeStruct((t, d), BF16),
        scratch_shapes=[pltpu.VMEM((ML_HEADS, dqk, dv), F32), pltpu.VMEM((ML_HEADS, dqk, LANES), F32),
                        pltpu.VMEM((F32_SUBLANES, LANES), F32)],
        compiler_params=_params(("arbitrary",)), name="mlstm",
    )(tril, norm, q, k, v, og, gates, gates_t)


def _swa_bias(blk):
    group = SW_HEADS // SW_KV_HEADS
    kpos = np.arange(2 * blk)[:, None] - blk
    dist = np.arange(blk)[None, :] - kpos
    slopes = 2.0 ** (-8.0 * (np.arange(SW_HEADS) + 1) / SW_HEADS)
    ok = (dist >= 0) & (dist < SW_WINDOW)
    tables = []
    for first in (True, False):
        valid = ok & ((kpos >= 0) | (not first))
        bias = np.where(valid[None], -slopes[:, None, None] * dist[None] * LOG2E, -np.inf)
        bias = bias.reshape(SW_KV_HEADS, group, 2 * blk, blk).transpose(0, 2, 1, 3)
        tables.append(bias.reshape(SW_KV_HEADS, 2 * blk, group * blk))
    return jnp.asarray(np.stack(tables), F32)


def _group_mean_sq(x, n):
    lanes = x.shape[1]
    same = (lax.broadcasted_iota(jnp.int32, (lanes, lanes), 0) // n
            == lax.broadcasted_iota(jnp.int32, (lanes, lanes), 1) // n)
    ones = jnp.where(same, 1.0, 0.0).astype(BF16)
    hi, lo = _split_bf16(x * x, 2)
    return (_dot(hi, ones) + _dot(lo, ones)) * (1.0 / n)


def _swa_kernel(qg_ref, kg_ref, sink_ref, bias_ref, q_ref, kp_ref, kc_ref, vp_ref, vc_ref, o_ref, *, blk):
    group = SW_HEADS // SW_KV_HEADS
    n_blk = q_ref.shape[0] // blk
    lo = lax.broadcasted_iota(jnp.int32, (1, LANES), 1) < SW_HD
    kk = jnp.concatenate([kp_ref[...], kc_ref[...]], axis=0)
    vv = jnp.concatenate([vp_ref[...], vc_ref[...]], axis=0)
    q_scale = qg_ref[...] * (SW_HD ** -0.5 * LOG2E)
    kn, v_low, v_high, qst = {}, {}, {}, {}
    for kg in range(SW_KV_HEADS // 2):
        kx = kk[:, kg * LANES:(kg + 1) * LANES]
        vx = vv[:, kg * LANES:(kg + 1) * LANES]
        kx = kx * lax.rsqrt(_group_mean_sq(kx, SW_HD) + EPS) * kg_ref[...]
        kr = pltpu.roll(kx, SW_HD, axis=1)
        vr = pltpu.roll(vx, SW_HD, axis=1)
        for half in range(2):
            kv = 2 * kg + half
            kn[kv] = (jnp.where(lo, kx, kr) if half == 0 else jnp.where(lo, kr, kx)).astype(BF16)
            v_low[kv] = jnp.where(lo, vx if half == 0 else vr, 0.0).astype(BF16)
            v_high[kv] = jnp.where(lo, 0.0, vr if half == 0 else vx).astype(BF16)
    pairs = [(b, kv) for b in range(n_blk) for kv in range(SW_KV_HEADS)]
    for b, kv in pairs:
        qs = []
        for j in range(kv * group // 2, (kv + 1) * group // 2):
            qx = q_ref[b * blk:(b + 1) * blk, j * LANES:(j + 1) * LANES]
            qn = qx * lax.rsqrt(_group_mean_sq(qx, SW_HD) + EPS) * q_scale
            qs += [jnp.where(lo, qn, 0.0).astype(BF16), jnp.where(lo, 0.0, qn).astype(BF16)]
        qst[b, kv] = jnp.concatenate(qs, axis=0)
    s = {(b, kv): _dot_nt(kn[kv][b * blk:(b + 2) * blk], qst[b, kv]) + bias_ref[0 if b == 0 else 1, kv]
         for b, kv in pairs}
    pn = {}
    for b, kv in pairs:
        sink = sink_ref[:, kv * group * blk:(kv + 1) * group * blk]
        mx = jnp.maximum(jnp.max(s[b, kv], axis=0, keepdims=True), sink)
        p = jnp.exp2(s[b, kv] - mx)
        inv = 1.0 / (jnp.sum(p, axis=0, keepdims=True) + jnp.exp2(sink - mx))
        pn[b, kv] = (p * inv).astype(BF16)
    for b, kv in pairs:
        ks = slice(b * blk, (b + 2) * blk)
        for idx, j in enumerate(range(kv * group // 2, (kv + 1) * group // 2)):
            even = pn[b, kv][:, 2 * idx * blk:(2 * idx + 1) * blk]
            odd = pn[b, kv][:, (2 * idx + 1) * blk:(2 * idx + 2) * blk]
            o_ref[b * blk:(b + 1) * blk, j * LANES:(j + 1) * LANES] = (
                _dot_tn(even, v_low[kv][ks]) + _dot_tn(odd, v_high[kv][ks])).astype(o_ref.dtype)


def _swa_call(q, k, v, q_gain, k_gain, sinks):
    t, nq = q.shape
    nk = k.shape[1]
    b = SW_BLOCK
    n_blk = SW_TILE // b
    bias = _swa_bias(b)
    qg2 = jnp.tile(q_gain, (1, LANES // SW_HD))
    kg2 = jnp.tile(k_gain, (1, LANES // SW_HD))
    sink_row = jnp.repeat(sinks, b, axis=1) * LOG2E
    cur = lambda n: pl.BlockSpec((SW_TILE, n), lambda s: (s, 0))
    prev = lambda n: pl.BlockSpec((b, n), lambda s: (jnp.maximum(s * n_blk - 1, 0), 0))
    bias2 = jnp.stack([bias, jnp.stack([bias[1], bias[1]])])
    bias_spec = pl.BlockSpec((None,) + bias2.shape[1:], lambda s: (jnp.minimum(s, 1), 0, 0, 0, 0))
    return pl.pallas_call(
        functools.partial(_swa_kernel, blk=b), grid=(t // SW_TILE,),
        in_specs=[_resident(qg2.shape), _resident(kg2.shape), _resident(sink_row.shape), bias_spec,
                  cur(nq), prev(nk), cur(nk), prev(nk), cur(nk)],
        out_specs=cur(nq), out_shape=jax.ShapeDtypeStruct((t, nq), BF16),
        compiler_params=_params(("arbitrary",)), name="swa",
    )(qg2, kg2, sink_row, bias2, q, k, k, v, v)


def kernel(x, p, norm_gains, w_ffn_gu, w_ffn_down, w_ple_gate, w_ple_proj, hg_lower_bounds, hg_w_in,
           hg_g_norm, hg_w_out, ml_w_qkvo, ml_w_if, ml_b_if, ml_norm, ml_w_out, sw_w_qkv, sw_q_norm,
           sw_k_norm, sw_sinks, sw_w_o):
    batch, seq, d = x.shape
    depth = p.shape[0]
    w_in = (hg_w_in, ml_w_qkvo, sw_w_qkv)
    w_out = (hg_w_out, ml_w_out, sw_w_o)

    def pre_jobs(layer):
        return [(w_ffn_gu, (layer, 0)), (w_ffn_down, (layer, 0)), (w_in[layer % N_MIXERS], (layer // N_MIXERS,))]

    def post_jobs(layer):
        return [(w_out[layer % N_MIXERS], (layer // N_MIXERS,)), (w_ffn_gu, (layer, 1)), (w_ffn_down, (layer, 1)),
                (w_ple_gate, (layer,)), (w_ple_proj, (layer,))]

    outs = []
    for bi in range(batch):
        xs, xb = x, bi
        pre_w = [w[lead].astype(BF16) for w, lead in pre_jobs(0)]
        for layer in range(depth):
            kind, j = layer % N_MIXERS, layer // N_MIXERS
            pre = functools.partial(_pre_call, xs, xb, norm_gains, layer, pre_w, cast_jobs=post_jobs(layer))
            if kind == 0:
                h, q, f, i, og, *post_w = pre((d, d, d, d), (F32, F32, F32, F32))
                mix = _hgrn2_call(q, f, i, og, hg_lower_bounds, hg_g_norm[j][None, :], layer)
            elif kind == 1:
                dqk = (ml_w_qkvo.shape[2] - 2 * d) // 2
                h, q, k, v, og, gates, gates_t, *post_w = pre((dqk, dqk, d, d), (F32, F32, F32, F32),
                                                              gate_w=ml_w_if[j], gate_b=ml_b_if[j])
                mix = _mlstm_call(q, k, v, og, gates, gates_t, ml_norm[j][None, :])
            else:
                nq = SW_HEADS * SW_HD
                nk = SW_KV_HEADS * SW_HD
                h, q, k, v, *post_w = pre((nq, nk, nk), (F32, F32, F32))
                mix = _swa_call(q, k, v, sw_q_norm[j][None, :], sw_k_norm[j][None, :], sw_sinks[j][None, :])
            jobs = pre_jobs(layer + 1) if layer + 1 < depth else []
            xs, *pre_w = _post_call(h, mix, p, bi, norm_gains, layer, post_w, cast_jobs=jobs)
            xb = 0
        outs.append(xs)
    return outs[0] if batch == 1 else jnp.concatenate(outs, axis=0)
```

```python
import functools

import numpy as np
import jax
import jax.numpy as jnp
from jax import lax
from jax.experimental import pallas as pl
from jax.experimental.pallas import tpu as pltpu

F32 = jnp.float32
BF16 = jnp.bfloat16

EPS = 1e-6
N_MIXERS = 3
HG_HEADS = 8
ML_HEADS = 4
ML_GATE_CAP = 15.0
SW_HEADS = 16
SW_KV_HEADS = 4
SW_HD = 64
SW_WINDOW = 128

V7X_VMEM_LIMIT_BYTES = 56 * 1024 * 1024
LANES = 128
BF16_SUBLANES = 16
F32_SUBLANES = 8
LOG2E = 1.4426950408889634

ROW_TILE = 512
FF_CHUNK = 256
HG_CHUNK = 128
HG_TILE = 256
ML_CHUNK = 256
SW_BLOCK = 128
SW_TILE = 256


def _dot(a, b):
    return jnp.dot(a, b, preferred_element_type=F32)


def _dot_nt(a, b):
    return lax.dot_general(a, b, (((1,), (1,)), ((), ())), preferred_element_type=F32)


def _dot_tn(a, b):
    return lax.dot_general(a, b, (((0,), (0,)), ((), ())), preferred_element_type=F32)


def _sigmoid(x):
    return 0.5 * jnp.tanh(0.5 * x) + 0.5


def _log_sigmoid(x):
    return jnp.minimum(x, 0.0) - jnp.log1p(jnp.exp(-jnp.abs(x)))


def _rms(x, gain):
    return x * lax.rsqrt(jnp.mean(x * x, axis=-1, keepdims=True) + EPS) * gain


def _split_bf16(x, terms):
    parts = []
    r = x
    for _ in range(terms):
        p = r.astype(BF16)
        parts.append(p)
        r = r - p.astype(F32)
    return parts


def _swiglu_residual(x, gain, wgu_ref, wd_ref):
    d_ff = wd_ref.shape[0]
    xn = _rms(x, gain).astype(BF16)
    y = jnp.zeros_like(x)
    for j in range(d_ff // FF_CHUNK):
        lo = j * FF_CHUNK
        g = _dot(xn, wgu_ref[:, lo:lo + FF_CHUNK])
        u = _dot(xn, wgu_ref[:, d_ff + lo:d_ff + lo + FF_CHUNK])
        a = (g * _sigmoid(g) * u).astype(BF16)
        y = y + _dot(a, wd_ref[lo:lo + FF_CHUNK, :])
    return x + 0.5 * y


def _resident(shape, lead=()):
    nd = len(shape)
    block = (None,) * len(lead) + tuple(shape[len(lead):])
    index = tuple(lead) + (0,) * (nd - len(lead))
    return pl.BlockSpec(block, lambda *_: index, pipeline_mode=pl.Buffered(1))


def _params(semantics):
    return pltpu.CompilerParams(dimension_semantics=semantics,
                                vmem_limit_bytes=V7X_VMEM_LIMIT_BYTES)


def _cast_specs(jobs, n_steps):
    in_specs, out_specs, out_shapes = [], [], []
    for w, lead in jobs:
        rows, cols = w.shape[len(lead):]
        k = next(k for k in (1, 2, 4, 8) if (rows * k) % (n_steps * BF16_SUBLANES) == 0)
        hb = rows * k // n_steps
        in_specs.append(pl.BlockSpec((None,) * len(lead) + (hb, cols),
                                     lambda i, lead=tuple(lead), k=k: lead + (i // k, 0)))
        out_specs.append(pl.BlockSpec((hb, cols), lambda i, k=k: (i // k, 0)))
        out_shapes.append(jax.ShapeDtypeStruct((rows, cols), BF16))
    return in_specs, out_specs, out_shapes


def _cast_slabs(in_refs, out_refs):
    for src, dst in zip(in_refs, out_refs):
        dst[...] = src[...].astype(dst.dtype)


def _pre_kernel(*refs, splits, n_gate, n_cast):
    n_in = 5 + (2 if n_gate else 0)
    ins, cast_in, outs = refs[:n_in], refs[n_in:n_in + n_cast], refs[n_in + n_cast:]
    x_ref, gains_ref, wgu_ref, wd_ref, win_ref = ins[:5]
    outs, cast_out = outs[:len(outs) - n_cast], outs[len(outs) - n_cast:]
    _cast_slabs(cast_in, cast_out)
    h_ref = outs[0]
    proj_refs = outs[1:1 + len(splits)]
    h = _swiglu_residual(x_ref[...], gains_ref[0:1, :], wgu_ref, wd_ref)
    h_ref[...] = h
    xn = _rms(h, gains_ref[1:2, :])
    xb = xn.astype(BF16)
    off = 0
    for r, n in zip(proj_refs, splits):
        r[...] = _dot(xb, win_ref[:, off:off + n]).astype(r.dtype)
        off += n
    if n_gate:
        wif_ref, bif_ref = ins[5:7]
        g_ref, gt_ref = outs[1 + len(splits):]
        xh, xl = _split_bf16(xn, 2)
        wh, wl = _split_bf16(wif_ref[...], 2)
        pre = _dot(xh, wh) + _dot(xl, wh) + _dot(xh, wl) + bif_ref[...]
        capped = ML_GATE_CAP * jnp.tanh(pre / ML_GATE_CAP)
        lane = lax.broadcasted_iota(jnp.int32, capped.shape, 1)
        gates = jnp.where(lane < n_gate // 2, capped, _log_sigmoid(capped))
        g_ref[...] = gates
        gt_ref[...] = gates.T[0:gt_ref.shape[0], :]


def _pre_call(x, bi, gains, layer, weights, splits, out_dtypes, gate_w=None, gate_b=None, cast_jobs=()):
    _, t, d = x.shape
    n_gate = 0 if gate_w is None else gate_w.shape[1]
    tm = ROW_TILE
    row = lambda n: pl.BlockSpec((tm, n), lambda i: (i, 0))
    in_specs = [pl.BlockSpec((None, tm, d), lambda i: (bi, i, 0)), _resident(gains.shape, (layer,))]
    in_specs += [_resident(w.shape) for w in weights]
    args = [x, gains, *weights]
    out_shape = [jax.ShapeDtypeStruct((t, d), F32)]
    out_specs = [row(d)]
    for n, dt in zip(splits, out_dtypes):
        out_shape.append(jax.ShapeDtypeStruct((t, n), dt))
        out_specs.append(row(n))
    if n_gate:
        wpad = jnp.zeros((d, LANES), F32).at[:, :n_gate].set(gate_w)
        bpad = jnp.zeros((1, LANES), F32).at[0, :n_gate].set(gate_b)
        in_specs += [_resident(wpad.shape), _resident(bpad.shape)]
        args += [wpad, bpad]
        out_shape += [jax.ShapeDtypeStruct((t, LANES), F32), jax.ShapeDtypeStruct((n_gate, t), F32)]
        out_specs += [row(LANES), pl.BlockSpec((n_gate, tm), lambda i: (0, i))]
    c_in, c_out, c_shape = _cast_specs(cast_jobs, t // tm)
    return pl.pallas_call(
        functools.partial(_pre_kernel, splits=tuple(splits), n_gate=n_gate, n_cast=len(cast_jobs)),
        grid=(t // tm,), in_specs=in_specs + c_in, out_specs=out_specs + c_out, out_shape=out_shape + c_shape,
        compiler_params=_params(("arbitrary",)), name="pre",
    )(*args, *[w for w, _ in cast_jobs])


def _post_kernel(*refs, n_cast):
    h_ref, o_ref, p_ref, gains_ref, wout_ref, wgu_ref, wd_ref, wpg_ref, wpp_ref = refs[:9]
    cast_in, x_ref, cast_out = refs[9:9 + n_cast], refs[9 + n_cast], refs[10 + n_cast:]
    _cast_slabs(cast_in, cast_out)
    h = h_ref[...] + _dot(o_ref[...], wout_ref[...])
    h = _swiglu_residual(h, gains_ref[2:3, :], wgu_ref, wd_ref)
    gate = _sigmoid(_dot(_rms(h, gains_ref[3:4, :]).astype(BF16), wpg_ref[...]))
    x_ref[...] = h + gate * _dot(p_ref[...].astype(BF16), wpp_ref[...])


def _post_call(h, o, p, bi, gains, layer, weights, cast_jobs=()):
    t, d = h.shape
    tm = ROW_TILE
    row = lambda n: pl.BlockSpec((tm, n), lambda i: (i, 0))
    c_in, c_out, c_shape = _cast_specs(cast_jobs, t // tm)
    return pl.pallas_call(
        functools.partial(_post_kernel, n_cast=len(cast_jobs)), grid=(t // tm,),
        in_specs=[row(d), row(o.shape[1]), pl.BlockSpec((None, None, tm, p.shape[3]), lambda i: (layer, bi, i, 0)),
                  _resident(gains.shape, (layer,))] + [_resident(w.shape) for w in weights] + c_in,
        out_specs=[pl.BlockSpec((None, tm, d), lambda i: (0, i, 0))] + c_out,
        out_shape=[jax.ShapeDtypeStruct((1, t, d), F32)] + c_shape,
        compiler_params=_params(("arbitrary",)), name="post",
    )(h, o, p, gains, *weights, *[w for w, _ in cast_jobs])


def _hgrn2_tables(c):
    n_lev = int(np.log2(c))
    assert 2 ** n_lev == c
    mats = np.zeros((n_lev, c, c), np.float32)
    r = np.arange(c)[:, None]
    j = np.arange(c)[None, :]
    mats[0] = j <= r
    for lev in range(n_lev - 1):
        half = 2 ** lev
        m = (r // (2 * half)) * (2 * half) + half
        mats[1 + lev] = np.where(r >= m, (j >= m) & (j <= r), (j > r) & (j <= m - 1))
    x = r ^ j
    lvl = np.where(r > j, np.floor(np.log2(np.maximum(x, 1))).astype(np.int32), -1)
    lvl = np.where(r == j, n_lev, lvl).astype(np.int32)
    mats = mats.reshape(-1, c)
    return jnp.asarray(np.concatenate([mats, mats], axis=1), BF16), jnp.asarray(lvl), n_lev


def _hgrn2_kernel(lbraw_ref, gn_ref, mexp_ref, lvl_ref, q_ref, f_ref, i_ref, og_ref, o_ref, st_ref,
                  *, layer, n_lev, heads, c):
    rows, d = q_ref.shape
    dk = d // heads
    chunks = range(rows // c)
    hh = range(heads)
    rs = {ci: slice(ci * c, (ci + 1) * c) for ci in chunks}
    sl = {h: slice(h * dk, (h + 1) * dk) for h in hh}
    pairs = [(ci, h) for ci in chunks for h in hh]

    @pl.when(pl.program_id(0) == 0)
    def _():
        st_ref[...] = jnp.zeros_like(st_ref)

    lrows = [lbraw_ref[r:r + 1, :] for r in range(lbraw_ref.shape[0])]
    mx = functools.reduce(jnp.maximum, lrows)
    es = [jnp.exp(r - mx) for r in lrows]
    lb = sum(es[1:layer + 1], jnp.zeros_like(mx)) / sum(es)

    key, expo = {}, {}
    for ci in chunks:
        f = f_ref[rs[ci], :]
        t = jnp.exp(-jnp.abs(f))
        r = 1.0 / (1.0 + t)
        pos = f >= 0.0
        tiny = jnp.logical_and(lb <= 0.0, jnp.logical_not(pos))
        num = jnp.where(pos, 1.0 + lb * t, lb + t)
        logf = jnp.log(jnp.where(tiny, 1.0, num) * r) + jnp.where(tiny, f, 0.0)
        key[ci] = (1.0 - lb) * jnp.where(pos, t * r, r)
        expo[ci] = _dot(mexp_ref[...], jnp.concatenate(_split_bf16(logf * LOG2E, 2), axis=0))

    lvl = lvl_ref[...]
    row = lax.broadcasted_iota(jnp.int32, (c, 1), 0)
    qh, kh, qb, kb, vh, bh, att, o = {}, {}, {}, {}, {}, {}, {}, {}
    for ci, h in pairs:
        qv = q_ref[rs[ci], sl[h]]
        qh[ci, h] = qv * _sigmoid(qv)
        kh[ci, h] = key[ci][:, sl[h]]
        qb[ci, h], kb[ci, h] = qh[ci, h].astype(BF16), kh[ci, h].astype(BF16)
        vh[ci, h] = i_ref[rs[ci], sl[h]].astype(BF16)
        bh[ci, h] = expo[ci][0:c, sl[h]]
    for ci, h in pairs:
        att[ci, h] = jnp.where(lvl == n_lev, jnp.sum(qh[ci, h] * kh[ci, h], axis=1, keepdims=True), 0.0)
    for lev in range(n_lev):
        upper = (row & (1 << lev)) != 0
        for ci, h in pairs:
            if lev == n_lev - 1:
                ex = -jnp.abs(bh[ci, h] - bh[ci, h][c // 2 - 1:c // 2, :])
            else:
                ex = expo[ci][(1 + lev) * c:(2 + lev) * c, sl[h]]
            e = jnp.exp2(ex.astype(BF16))
            x = jnp.where(upper, qb[ci, h], kb[ci, h]) * e
            att[ci, h] = jnp.where(lvl == lev, _dot_nt(x, x), att[ci, h])
    st = {h: st_ref[h] for h in hh}
    for ci in chunks:
        for h in hh:
            o[ci, h] = (_dot_nt(qb[ci, h] * jnp.exp2(bh[ci, h].astype(BF16)), st[h].astype(BF16))
                        + _dot(att[ci, h].astype(BF16), vh[ci, h]))
        for h in hh:
            khat = kb[ci, h] * jnp.exp2((bh[ci, h][c - 1:c, :] - bh[ci, h]).astype(BF16))
            st[h] = st[h] * jnp.exp2(bh[ci, h][c - 1:c, :]) + _dot_tn(vh[ci, h], khat)
    for h in hh:
        st_ref[h] = st[h]
    for ci, h in pairs:
        ogv = og_ref[rs[ci], sl[h]]
        o_ref[rs[ci], sl[h]] = (_rms(o[ci, h], gn_ref[...]) * (ogv * _sigmoid(ogv))).astype(o_ref.dtype)


def _hgrn2_call(q, f, i, og, lbraw, g_norm, layer):
    t, d = q.shape
    c = HG_CHUNK
    tile = HG_TILE
    mexp, lvl, n_lev = _hgrn2_tables(c)
    dk = d // HG_HEADS
    blk = pl.BlockSpec((tile, d), lambda n: (n, 0))
    return pl.pallas_call(
        functools.partial(_hgrn2_kernel, layer=layer, n_lev=n_lev, heads=HG_HEADS, c=c),
        grid=(t // tile,),
        in_specs=[_resident(lbraw.shape), _resident(g_norm.shape), _resident(mexp.shape),
                  _resident(lvl.shape), blk, blk, blk, blk],
        out_specs=blk, out_shape=jax.ShapeDtypeStruct((t, d), BF16),
        scratch_shapes=[pltpu.VMEM((HG_HEADS, dk, dk), F32)],
        compiler_params=_params(("arbitrary",)), name="hgrn2",
    )(lbraw, g_norm, mexp, lvl, q, f, i, og)


def _mlstm_kernel(tril_ref, norm_ref, q_ref, k_ref, v_ref, og_ref, g_ref, gt_ref, o_ref,
                  c_ref, n_ref, m_ref, *, heads):
    c = q_ref.shape[0]
    dqk = q_ref.shape[1] // heads
    dv = v_ref.shape[1] // heads

    @pl.when(pl.program_id(0) == 0)
    def _():
        c_ref[...] = jnp.zeros_like(c_ref)
        n_ref[...] = jnp.zeros_like(n_ref)
        m_ref[...] = jnp.zeros_like(m_ref)

    tril = tril_ref[...]
    gates = g_ref[...]
    gates_t = gt_ref[...]
    cum_col = sum(_dot(tril, part) for part in _split_bf16(gates, 3))
    cum_row = sum(_dot_nt(part, tril) for part in _split_bf16(gates_t, 3))
    causal = lax.broadcasted_iota(jnp.int32, (c, c), 0) >= lax.broadcasted_iota(jnp.int32, (c, c), 1)

    hh = range(heads)
    ones = jnp.ones((c, LANES), BF16)
    wide = lambda a, n: jnp.concatenate([a] * (n // LANES), axis=1)
    b_col = {h: cum_col[:, heads + h:heads + h + 1] for h in hh}
    b_row = {h: cum_row[heads + h:heads + h + 1, :] for h in hh}
    i_col = {h: gates[:, h:h + 1] for h in hh}
    i_row = {h: gates_t[h:h + 1, :] for h in hh}
    m_prev = {h: m_ref[h:h + 1, 0:1] for h in hh}
    qb = {h: q_ref[:, h * dqk:(h + 1) * dqk].astype(BF16) for h in hh}
    kh = {h: k_ref[:, h * dqk:(h + 1) * dqk] * (dqk ** -0.5) for h in hh}
    vh = {h: v_ref[:, h * dv:(h + 1) * dv].astype(BF16) for h in hh}
    cst = {h: c_ref[h] for h in hh}
    nst = {h: n_ref[h] for h in hh}

    b_rep = {h: jnp.broadcast_to(b_col[h], (c, LANES)) for h in hh}
    log_intra = {h: jnp.where(causal, wide(b_rep[h], c) - (b_row[h] - i_row[h]), -jnp.inf) for h in hh}
    m_col = {h: jnp.maximum(jnp.max(log_intra[h], axis=1, keepdims=True), b_col[h] + m_prev[h]) for h in hh}
    m_rep = {h: jnp.broadcast_to(m_col[h], (c, LANES)) for h in hh}
    w_intra = {h: jnp.exp(log_intra[h] - wide(m_rep[h], c)) for h in hh}
    w_inter = {h: jnp.exp(b_rep[h] + m_prev[h] - m_rep[h]) for h in hh}
    scores = {h: (_dot_nt(qb[h], kh[h].astype(BF16)) * w_intra[h]).astype(BF16) for h in hh}
    inter = {h: _dot(qb[h], cst[h].astype(BF16)) for h in hh}
    num = {h: wide(w_inter[h], dv) * inter[h] + _dot(scores[h], vh[h]) for h in hh}
    den = {h: w_inter[h] * _dot(qb[h], nst[h].astype(BF16)) + _dot(scores[h], ones) for h in hh}
    inv = {h: 1.0 / jnp.maximum(jnp.abs(den[h]), jnp.exp(-m_rep[h])) for h in hh}
    hid = {h: num[h] * wide(inv[h], dv) for h in hh}

    b_last = {h: b_col[h][c - 1:c, :] for h in hh}
    log_state = {h: b_last[h] + m_prev[h] for h in hh}
    m_new = {h: jnp.maximum(log_state[h], jnp.max(b_last[h] - b_row[h] + i_row[h], axis=1, keepdims=True))
             for h in hh}
    w_src = {h: jnp.broadcast_to(jnp.exp(b_last[h] - b_col[h] + i_col[h] - m_new[h]), (c, LANES)) for h in hh}
    decay = {h: jnp.exp(log_state[h] - m_new[h]) for h in hh}
    khat = {h: (kh[h] * w_src[h]).astype(BF16) for h in hh}
    for h in hh:
        c_ref[h] = decay[h] * cst[h] + _dot_tn(khat[h], vh[h])
        n_ref[h] = decay[h] * nst[h] + _dot_tn(khat[h], ones)
        m_ref[h:h + 1, :] = jnp.broadcast_to(m_new[h], (1, m_ref.shape[1]))
    ones_dv = jnp.ones((dv, LANES), BF16)
    for h in hh:
        sq_hi, sq_lo = _split_bf16(hid[h] * hid[h], 2)
        ms = (_dot(sq_hi, ones_dv) + _dot(sq_lo, ones_dv)) * (1.0 / dv)
        ogv = og_ref[:, h * dv:(h + 1) * dv]
        o_ref[:, h * dv:(h + 1) * dv] = (hid[h] * wide(lax.rsqrt(ms + EPS), dv) * norm_ref[:, h * dv:(h + 1) * dv]
                                         * _sigmoid(ogv)).astype(o_ref.dtype)


def _mlstm_call(q, k, v, og, gates, gates_t, norm):
    t = q.shape[0]
    d = v.shape[1]
    c = ML_CHUNK
    dqk = q.shape[1] // ML_HEADS
    dv = d // ML_HEADS
    tril = jnp.asarray(np.tril(np.ones((c, c), np.float32)), BF16)
    blk = lambda n: pl.BlockSpec((c, n), lambda s: (s, 0))
    return pl.pallas_call(
        functools.partial(_mlstm_kernel, heads=ML_HEADS),
        grid=(t // c,),
        in_specs=[_resident(tril.shape), _resident(norm.shape), blk(q.shape[1]), blk(k.shape[1]),
                  blk(d), blk(d), blk(LANES), pl.BlockSpec((2 * ML_HEADS, c), lambda s: (0, s))],
        out_specs=blk(d), out_shape=jax.ShapeDtypeStruct((t, d), BF16),
        scratch_shapes=[pltpu.VMEM((ML_HEADS, dqk, dv), F32), pltpu.VMEM((ML_HEADS, dqk, LANES), F32),
                        pltpu.VMEM((F32_SUBLANES, LANES), F32)],
        compiler_params=_params(("arbitrary",)), name="mlstm",
    )(tril, norm, q, k, v, og, gates, gates_t)


def _swa_bias(blk):
    group = SW_HEADS // SW_KV_HEADS
    kpos = np.arange(2 * blk)[:, None] - blk
    dist = np.arange(blk)[None, :] - kpos
    slopes = 2.0 ** (-8.0 * (np.arange(SW_HEADS) + 1) / SW_HEADS)
    ok = (dist >= 0) & (dist < SW_WINDOW)
    tables = []
    for first in (True, False):
        valid = ok & ((kpos >= 0) | (not first))
        bias = np.where(valid[None], -slopes[:, None, None] * dist[None] * LOG2E, -np.inf)
        bias = bias.reshape(SW_KV_HEADS, group, 2 * blk, blk).transpose(0, 2, 1, 3)
        tables.append(bias.reshape(SW_KV_HEADS, 2 * blk, group * blk))
    return jnp.asarray(np.stack(tables), F32)


def _group_mean_sq(x, n):
    lanes = x.shape[1]
    same = (lax.broadcasted_iota(jnp.int32, (lanes, lanes), 0) // n
            == lax.broadcasted_iota(jnp.int32, (lanes, lanes), 1) // n)
    ones = jnp.where(same, 1.0, 0.0).astype(BF16)
    hi, lo = _split_bf16(x * x, 2)
    return (_dot(hi, ones) + _dot(lo, ones)) * (1.0 / n)


def _swa_kernel(qg_ref, kg_ref, sink_ref, bias_ref, q_ref, kp_ref, kc_ref, vp_ref, vc_ref, o_ref, *, blk):
    group = SW_HEADS // SW_KV_HEADS
    n_blk = q_ref.shape[0] // blk
    lo = lax.broadcasted_iota(jnp.int32, (1, LANES), 1) < SW_HD
    kk = jnp.concatenate([kp_ref[...], kc_ref[...]], axis=0)
    vv = jnp.concatenate([vp_ref[...], vc_ref[...]], axis=0)
    q_scale = qg_ref[...] * (SW_HD ** -0.5 * LOG2E)
    kn, v_low, v_high, qst = {}, {}, {}, {}
    for kg in range(SW_KV_HEADS // 2):
        kx = kk[:, kg * LANES:(kg + 1) * LANES]
        vx = vv[:, kg * LANES:(kg + 1) * LANES]
        kx = kx * lax.rsqrt(_group_mean_sq(kx, SW_HD) + EPS) * kg_ref[...]
        kr = pltpu.roll(kx, SW_HD, axis=1)
        vr = pltpu.roll(vx, SW_HD, axis=1)
        for half in range(2):
            kv = 2 * kg + half
            kn[kv] = (jnp.where(lo, kx, kr) if half == 0 else jnp.where(lo, kr, kx)).astype(BF16)
            v_low[kv] = jnp.where(lo, vx if half == 0 else vr, 0.0).astype(BF16)
            v_high[kv] = jnp.where(lo, 0.0, vr if half == 0 else vx).astype(BF16)
    pairs = [(b, kv) for b in range(n_blk) for kv in range(SW_KV_HEADS)]
    for b, kv in pairs:
        qs = []
        for j in range(kv * group // 2, (kv + 1) * group // 2):
            qx = q_ref[b * blk:(b + 1) * blk, j * LANES:(j + 1) * LANES]
            qn = qx * lax.rsqrt(_group_mean_sq(qx, SW_HD) + EPS) * q_scale
            qs += [jnp.where(lo, qn, 0.0).astype(BF16), jnp.where(lo, 0.0, qn).astype(BF16)]
        qst[b, kv] = jnp.concatenate(qs, axis=0)
    s = {(b, kv): _dot_nt(kn[kv][b * blk:(b + 2) * blk], qst[b, kv]) + bias_ref[0 if b == 0 else 1, kv]
         for b, kv in pairs}
    pn = {}
    for b, kv in pairs:
        sink = sink_ref[:, kv * group * blk:(kv + 1) * group * blk]
        mx = jnp.maximum(jnp.max(s[b, kv], axis=0, keepdims=True), sink)
        p = jnp.exp2(s[b, kv] - mx)
        inv = 1.0 / (jnp.sum(p, axis=0, keepdims=True) + jnp.exp2(sink - mx))
        pn[b, kv] = (p * inv).astype(BF16)
    for b, kv in pairs:
        ks = slice(b * blk, (b + 2) * blk)
        for idx, j in enumerate(range(kv * group // 2, (kv + 1) * group // 2)):
            even = pn[b, kv][:, 2 * idx * blk:(2 * idx + 1) * blk]
            odd = pn[b, kv][:, (2 * idx + 1) * blk:(2 * idx + 2) * blk]
            o_ref[b * blk:(b + 1) * blk, j * LANES:(j + 1) * LANES] = (
                _dot_tn(even, v_low[kv][ks]) + _dot_tn(odd, v_high[kv][ks])).astype(o_ref.dtype)


def _swa_call(q, k, v, q_gain, k_gain, sinks):
    t, nq = q.shape
    nk = k.shape[1]
    b = SW_BLOCK
    n_blk = SW_TILE // b
    bias = _swa_bias(b)
    qg2 = jnp.tile(q_gain, (1, LANES // SW_HD))
    kg2 = jnp.tile(k_gain, (1, LANES // SW_HD))
    sink_row = jnp.repeat(sinks, b, axis=1) * LOG2E
    cur = lambda n: pl.BlockSpec((SW_TILE, n), lambda s: (s, 0))
    prev = lambda n: pl.BlockSpec((b, n), lambda s: (jnp.maximum(s * n_blk - 1, 0), 0))
    bias2 = jnp.stack([bias, jnp.stack([bias[1], bias[1]])])
    bias_spec = pl.BlockSpec((None,) + bias2.shape[1:], lambda s: (jnp.minimum(s, 1), 0, 0, 0, 0))
    return pl.pallas_call(
        functools.partial(_swa_kernel, blk=b), grid=(t // SW_TILE,),
        in_specs=[_resident(qg2.shape), _resident(kg2.shape), _resident(sink_row.shape), bias_spec,
                  cur(nq), prev(nk), cur(nk), prev(nk), cur(nk)],
        out_specs=cur(nq), out_shape=jax.ShapeDtypeStruct((t, nq), BF16),
        compiler_params=_params(("arbitrary",)), name="swa",
    )(qg2, kg2, sink_row, bias2, q, k, k, v, v)


def kernel(x, p, norm_gains, w_ffn_gu, w_ffn_down, w_ple_gate, w_ple_proj, hg_lower_bounds, hg_w_in,
           hg_g_norm, hg_w_out, ml_w_qkvo, ml_w_if, ml_b_if, ml_norm, ml_w_out, sw_w_qkv, sw_q_norm,
           sw_k_norm, sw_sinks, sw_w_o):
    batch, seq, d = x.shape
    depth = p.shape[0]
    w_in = (hg_w_in, ml_w_qkvo, sw_w_qkv)
    w_out = (hg_w_out, ml_w_out, sw_w_o)

    def pre_jobs(layer):
        return [(w_ffn_gu, (layer, 0)), (w_ffn_down, (layer, 0)), (w_in[layer % N_MIXERS], (layer // N_MIXERS,))]

    def post_jobs(layer):
        return [(w_out[layer % N_MIXERS], (layer // N_MIXERS,)), (w_ffn_gu, (layer, 1)), (w_ffn_down, (layer, 1)),
                (w_ple_gate, (layer,)), (w_ple_proj, (layer,))]

    outs = []
    for bi in range(batch):
        xs, xb = x, bi
        pre_w = [w[lead].astype(BF16) for w, lead in pre_jobs(0)]
        for layer in range(depth):
            kind, j = layer % N_MIXERS, layer // N_MIXERS
            pre = functools.partial(_pre_call, xs, xb, norm_gains, layer, pre_w, cast_jobs=post_jobs(layer))
            if kind == 0:
                h, q, f, i, og, *post_w = pre((d, d, d, d), (F32, F32, F32, F32))
                mix = _hgrn2_call(q, f, i, og, hg_lower_bounds, hg_g_norm[j][None, :], layer)
            elif kind == 1:
                dqk = (ml_w_qkvo.shape[2] - 2 * d) // 2
                h, q, k, v, og, gates, gates_t, *post_w = pre((dqk, dqk, d, d), (F32, F32, F32, F32),
                                                              gate_w=ml_w_if[j], gate_b=ml_b_if[j])
                mix = _mlstm_call(q, k, v, og, gates, gates_t, ml_norm[j][None, :])
            else:
                nq = SW_HEADS * SW_HD
                nk = SW_KV_HEADS * SW_HD
                h, q, k, v, *post_w = pre((nq, nk, nk), (F32, F32, F32))
                mix = _swa_call(q, k, v, sw_q_norm[j][None, :], sw_k_norm[j][None, :], sw_sinks[j][None, :])
            jobs = pre_jobs(layer + 1) if layer + 1 < depth else []
            xs, *pre_w = _post_call(h, mix, p, bi, norm_gains, layer, post_w, cast_jobs=jobs)
            xb = 0
        outs.append(xs)
    return outs[0] if batch == 1 else jnp.concatenate(outs, axis=0)
```
